```python
import math
import jax, jax.numpy as jnp
from jax import lax
import numpy as np

D_MODEL = 4096
BATCH = 1
SEQ = 8192
DEPTH = 4

N_MIXERS = 3
N_RWKV = (DEPTH + 2) // 3
N_SWA = (DEPTH + 1) // 3
N_CONV = DEPTH // 3

RWKV_HEAD = 64
RWKV_HEADS = D_MODEL // RWKV_HEAD
D_DECAY_LORA = max(32, int(round(1.8 * D_MODEL ** 0.5 / 32)) * 32)
D_AAA_LORA = max(32, int(round(1.8 * D_MODEL ** 0.5 / 32)) * 32)
D_MV_LORA = max(32, int(round(1.3 * D_MODEL ** 0.5 / 32)) * 32)
D_GATE_LORA = max(32, int(round(0.6 * D_MODEL ** 0.8 / 32)) * 32)
GN_EPS = 64e-5

SWA_HEAD_DIM = 64
SWA_Q_HEADS = D_MODEL // SWA_HEAD_DIM
SWA_KV_HEADS = 8
SWA_GROUP = SWA_Q_HEADS // SWA_KV_HEADS
WINDOW = 128
SWA_Q_W = SWA_Q_HEADS * SWA_HEAD_DIM
SWA_KV_W = SWA_KV_HEADS * SWA_HEAD_DIM
SWA_QKV = SWA_Q_W + 2 * SWA_KV_W
SWA_SCALE = SWA_HEAD_DIM ** -0.5
NEG_INF = -1e30

CONV_WIDTH = 31
LN_EPS = 1e-5

D_FF = 4 * D_MODEL
RMS_EPS = 1e-5

kernel_name = 'hybrid_rwkv7_swa_sink_conformer_trunk'


def rmsnorm(x, g):
    xf = x.astype(jnp.float32)
    y = xf * lax.rsqrt(jnp.mean(xf * xf, axis=-1, keepdims=True) + RMS_EPS)
    return (y * g.astype(jnp.float32)).astype(x.dtype)


def layernorm(x, g, b):
    xf = x.astype(jnp.float32)
    mu = jnp.mean(xf, axis=-1, keepdims=True)
    var = jnp.mean(jnp.square(xf - mu), axis=-1, keepdims=True)
    y = (xf - mu) * lax.rsqrt(var + LN_EPS)
    return (y * g.astype(jnp.float32) + b.astype(jnp.float32)).astype(x.dtype)


def _wkv7_scan(r, decay, k, v, a_vec, b_vec):
    B, T, H, N = r.shape

    def step(S, inp):
        r_t, w_t, k_t, v_t, a_t, b_t = inp
        sa = jnp.einsum('bhvk,bhk->bhv', S, a_t)
        S = S * w_t[:, :, None, :] + sa[..., None] * b_t[:, :, None, :] + v_t[..., None] * k_t[:, :, None, :]
        return S, jnp.einsum('bhvk,bhk->bhv', S, r_t)

    seq = tuple(jnp.swapaxes(t, 0, 1) for t in (r, decay, k, v, a_vec, b_vec))
    S0 = jnp.zeros((B, H, N, N), jnp.float32)
    _, y = lax.scan(step, S0, seq)
    return jnp.swapaxes(y, 0, 1)


def rwkv7_time_mix(h, v_first, vres, mix, w_rkv, w0, w1, w2, a0, a1, a2, g1, g2, k_k, k_a, r_k, lnx_g, lnx_b, w_o):
    B, T, D = h.shape
    H, N = RWKV_HEADS, RWKV_HEAD
    xx = jnp.pad(h, ((0, 0), (1, 0), (0, 0)))[:, :-1] - h
    xs = h[None] + xx[None] * mix[:, None, None, :]
    rkv = jnp.einsum('cbtd,cde->cbte', xs[:3], w_rkv)
    r, k, v = rkv[0], rkv[1], rkv[2]
    xv, xw, xa, xg = xs[2], xs[3], xs[4], xs[5]
    w = -jax.nn.softplus(-(w0 + jnp.tanh(xw @ w1) @ w2)) - 0.5
    a = jax.nn.sigmoid(a0 + (xa @ a1) @ a2)
    g = jax.nn.sigmoid(xg @ g1) @ g2
    if vres is None:
        v_first = v
    else:
        v0, v1, v2 = vres
        v = v + (v_first - v) * jax.nn.sigmoid(v0 + (xv @ v1) @ v2)
    heads = lambda t: t.reshape(B, T, H, N)
    kk = heads(k * k_k).astype(jnp.float32)
    kk = kk / jnp.maximum(jnp.linalg.norm(kk, axis=-1, keepdims=True), 1e-12)
    k = k * (1 + (a - 1) * k_a)
    rh, kh, vh, ah = heads(r), heads(k), heads(v), heads(a)
    decay = jnp.exp(-jnp.exp(heads(w).astype(jnp.float32)))
    y = _wkv7_scan(rh.astype(jnp.float32), decay, kh.astype(jnp.float32), vh.astype(jnp.float32),
                   -kk, kk * ah.astype(jnp.float32))
    mu = jnp.mean(y, axis=-1, keepdims=True)
    var = jnp.mean(jnp.square(y - mu), axis=-1, keepdims=True)
    yn = ((y - mu) * lax.rsqrt(var + GN_EPS)).reshape(B, T, D)
    yn = (yn * lnx_g.astype(jnp.float32) + lnx_b.astype(jnp.float32)).astype(h.dtype)
    bonus = jnp.sum(rh * kh * r_k, axis=-1, keepdims=True) * vh
    out = ((yn + bonus.reshape(B, T, D)) * g) @ w_o
    return out, v_first


def _with_prev_block(t):
    prev = jnp.concatenate([jnp.zeros_like(t[:, :1]), t[:, :-1]], axis=1)
    return jnp.concatenate([prev, t], axis=2)


def swa_sink_attention(h, w_qkv, b_qkv, sinks, w_o, b_o):
    B, T, _ = h.shape
    nb = T // WINDOW
    qkv = h @ w_qkv + b_qkv
    q = qkv[..., :SWA_Q_W].reshape(B, nb, WINDOW, SWA_KV_HEADS, SWA_GROUP, SWA_HEAD_DIM)
    k = qkv[..., SWA_Q_W:SWA_Q_W + SWA_KV_W].reshape(B, nb, WINDOW, SWA_KV_HEADS, SWA_HEAD_DIM)
    v = qkv[..., SWA_Q_W + SWA_KV_W:].reshape(B, nb, WINDOW, SWA_KV_HEADS, SWA_HEAD_DIM)
    kc, vc = _with_prev_block(k), _with_prev_block(v)
    s = jnp.einsum('bnqhgd,bnkhd->bnhgqk', q, kc).astype(jnp.float32) * SWA_SCALE
    qi = jnp.arange(WINDOW)[:, None]
    kj = jnp.arange(2 * WINDOW)[None, :]
    band = (kj > qi) & (kj <= qi + WINDOW)
    has_prev = (jnp.arange(nb) > 0)[:, None, None] | (kj >= WINDOW)[None]
    mask = band[None] & has_prev
    s = jnp.where(mask[None, :, None, None], s, NEG_INF)
    sink = jnp.broadcast_to(sinks.astype(jnp.float32).reshape(1, 1, SWA_KV_HEADS, SWA_GROUP, 1, 1),
                            s.shape[:-1] + (1,))
    p = jax.nn.softmax(jnp.concatenate([s, sink], axis=-1), axis=-1)[..., :-1]
    o = jnp.einsum('bnhgqk,bnkhd->bnqhgd', p.astype(vc.dtype), vc).reshape(B, T, SWA_Q_W)
    return o @ w_o + b_o


def conformer_conv(h, w_pw1, b_pw1, w_dw, b_dw, ln_g, ln_b, w_pw2, b_pw2):
    D = h.shape[-1]
    u = h @ w_pw1 + b_pw1
    u = u[..., :D] * jax.nn.sigmoid(u[..., D:])
    u = lax.conv_general_dilated(u, w_dw[:, None, :].astype(u.dtype), window_strides=(1,),
                                 padding=[(CONV_WIDTH - 1, 0)],
                                 dimension_numbers=('NWC', 'WIO', 'NWC'),
                                 feature_group_count=D) + b_dw
    u = jax.nn.silu(layernorm(u, ln_g, ln_b))
    return u @ w_pw2 + b_pw2


def squared_relu_mlp(h, w_in, w_out):
    return jnp.square(jax.nn.relu(h @ w_in)) @ w_out


def setup_inputs(seed: int = 0) -> dict:
    key = jax.random.key(seed)
    keys = jax.random.split(key, 48)
    ctr = [0]

    def nk():
        ctr[0] += 1
        return keys[ctr[0] - 1]

    f32 = jnp.float32
    nrm = lambda shape, scale: jax.random.normal(nk(), shape, f32) * scale
    unif = lambda shape, lo, hi: jax.random.uniform(nk(), shape, f32, lo, hi)
    D = D_MODEL
    NA, NS, NC = N_RWKV, N_SWA, N_CONV
    NV = max(N_RWKV - 1, 0)
    return {
        'x': nrm((BATCH, SEQ, D), 1.0),
        'norm_mix_g': 1.0 + nrm((DEPTH, D), 0.02),
        'norm_mlp_g': 1.0 + nrm((DEPTH, D), 0.02),
        'norm_f_g': 1.0 + nrm((D,), 0.02),
        'rwkv_mix': unif((NA, 6, D), 0.0, 1.0),
        'rwkv_w_rkv': nrm((NA, 3, D, D), D ** -0.5),
        'rwkv_w0': unif((NA, D), -6.0, -1.0),
        'rwkv_w1': nrm((NA, D, D_DECAY_LORA), D ** -0.5),
        'rwkv_w2': nrm((NA, D_DECAY_LORA, D), 0.1 * D_DECAY_LORA ** -0.5),
        'rwkv_a0': nrm((NA, D), 0.1),
        'rwkv_a1': nrm((NA, D, D_AAA_LORA), D ** -0.5),
        'rwkv_a2': nrm((NA, D_AAA_LORA, D), 0.1 * D_AAA_LORA ** -0.5),
        'rwkv_v0': nrm((NV, D), 0.1),
        'rwkv_v1': nrm((NV, D, D_MV_LORA), D ** -0.5),
        'rwkv_v2': nrm((NV, D_MV_LORA, D), 0.1 * D_MV_LORA ** -0.5),
        'rwkv_g1': nrm((NA, D, D_GATE_LORA), D ** -0.5),
        'rwkv_g2': nrm((NA, D_GATE_LORA, D), D_GATE_LORA ** -0.5),
        'rwkv_k_k': 0.85 + nrm((NA, D), 0.02),
        'rwkv_k_a': 1.0 + nrm((NA, D), 0.02),
        'rwkv_r_k': nrm((NA, RWKV_HEADS, RWKV_HEAD), 0.1),
        'rwkv_lnx_g': 1.0 + nrm((NA, D), 0.02),
        'rwkv_lnx_b': nrm((NA, D), 0.02),
        'rwkv_w_o': nrm((NA, D, D), D ** -0.5),
        'swa_w_qkv': nrm((NS, D, SWA_QKV), D ** -0.5),
        'swa_b_qkv': nrm((NS, SWA_QKV), 0.02),
        'swa_sinks': nrm((NS, SWA_Q_HEADS), 0.5),
        'swa_w_o': nrm((NS, SWA_Q_W, D), SWA_Q_W ** -0.5),
        'swa_b_o': nrm((NS, D), 0.02),
        'conv_w_pw1': nrm((NC, D, 2 * D), D ** -0.5),
        'conv_b_pw1': nrm((NC, 2 * D), 0.02),
        'conv_w_dw': nrm((NC, CONV_WIDTH, D), CONV_WIDTH ** -0.5),
        'conv_b_dw': nrm((NC, D), 0.02),
        'conv_ln_g': 1.0 + nrm((NC, D), 0.02),
        'conv_ln_b': nrm((NC, D), 0.02),
        'conv_w_pw2': nrm((NC, D, D), D ** -0.5),
        'conv_b_pw2': nrm((NC, D), 0.02),
        'mlp_w_in': nrm((DEPTH, D, D_FF), D ** -0.5),
        'mlp_w_out': nrm((DEPTH, D_FF, D), D_FF ** -0.5),
    }


def reference(x, norm_mix_g, norm_mlp_g, norm_f_g,
              rwkv_mix, rwkv_w_rkv, rwkv_w0, rwkv_w1, rwkv_w2, rwkv_a0, rwkv_a1, rwkv_a2,
              rwkv_v0, rwkv_v1, rwkv_v2, rwkv_g1, rwkv_g2, rwkv_k_k, rwkv_k_a, rwkv_r_k,
              rwkv_lnx_g, rwkv_lnx_b, rwkv_w_o,
              swa_w_qkv, swa_b_qkv, swa_sinks, swa_w_o, swa_b_o,
              conv_w_pw1, conv_b_pw1, conv_w_dw, conv_b_dw, conv_ln_g, conv_ln_b, conv_w_pw2, conv_b_pw2,
              mlp_w_in, mlp_w_out):
    ia = ib = ic = 0
    v_first = None
    for i in range(DEPTH):
        h = rmsnorm(x, norm_mix_g[i])
        kind = i % N_MIXERS
        if kind == 0:
            vres = None if ia == 0 else (rwkv_v0[ia - 1], rwkv_v1[ia - 1], rwkv_v2[ia - 1])
            out, v_first = rwkv7_time_mix(h, v_first, vres, rwkv_mix[ia], rwkv_w_rkv[ia], rwkv_w0[ia],
                                          rwkv_w1[ia], rwkv_w2[ia], rwkv_a0[ia], rwkv_a1[ia], rwkv_a2[ia],
                                          rwkv_g1[ia], rwkv_g2[ia], rwkv_k_k[ia], rwkv_k_a[ia], rwkv_r_k[ia],
                                          rwkv_lnx_g[ia], rwkv_lnx_b[ia], rwkv_w_o[ia])
            ia += 1
        elif kind == 1:
            out = swa_sink_attention(h, swa_w_qkv[ib], swa_b_qkv[ib], swa_sinks[ib], swa_w_o[ib], swa_b_o[ib])
            ib += 1
        else:
            out = conformer_conv(h, conv_w_pw1[ic], conv_b_pw1[ic], conv_w_dw[ic], conv_b_dw[ic],
                                 conv_ln_g[ic], conv_ln_b[ic], conv_w_pw2[ic], conv_b_pw2[ic])
            ic += 1
        x = x + out
        x = x + squared_relu_mlp(rmsnorm(x, norm_mlp_g[i]), mlp_w_in[i], mlp_w_out[i])
    return rmsnorm(x, norm_f_g)
```

```python
import functools

import jax
import jax.numpy as jnp
from jax import lax
from jax.experimental import pallas as pl
from jax.experimental.pallas import tpu as pltpu

F32 = jnp.float32
BF16 = jnp.bfloat16

LANES = 128
HEAD = 64
CHUNK = 128
WINDOW = 128
SWA_KV_HEADS = 8
CONV_WIDTH = 31
CONV_HALO = 32
RMS_EPS = 1e-5
LN_EPS = 1e-5
GN_EPS = 64e-5
NEG_INF = -1e30
VMEM_LIMIT = 48 * 1024 * 1024


def _cparams(sem):
    return pltpu.CompilerParams(dimension_semantics=sem, vmem_limit_bytes=VMEM_LIMIT)


def _tile(n, pref):
    if n <= pref:
        return n
    t = pref
    while n % t:
        t //= 2
    return t


def _rmsnorm_kernel(x_ref, g_ref, o_ref):
    x = x_ref[...]
    y = x * lax.rsqrt(jnp.mean(x * x, axis=-1, keepdims=True) + RMS_EPS)
    o_ref[...] = (y * g_ref[...]).astype(o_ref.dtype)


def _rmsnorm(x, g, out_dtype):
    m, d = x.shape
    tm = _tile(m, 256)
    return pl.pallas_call(
        _rmsnorm_kernel,
        grid=(m // tm,),
        in_specs=[pl.BlockSpec((tm, d), lambda i: (i, 0)),
                  pl.BlockSpec((1, d), lambda i: (0, 0))],
        out_specs=pl.BlockSpec((tm, d), lambda i: (i, 0)),
        out_shape=jax.ShapeDtypeStruct((m, d), out_dtype),
        compiler_params=_cparams(("arbitrary",)),
    )(x, g.reshape(1, d))


def _rwkv_mix_kernel(x_ref, g_ref, mix_ref, o_ref, carry_ref):
    @pl.when(pl.program_id(0) == 0)
    def _():
        carry_ref[...] = jnp.zeros_like(carry_ref)

    x = x_ref[...]
    h = x * lax.rsqrt(jnp.mean(x * x, axis=-1, keepdims=True) + RMS_EPS) * g_ref[...]
    tm = h.shape[0]
    prev = pltpu.roll(h, shift=1, axis=0)
    row = lax.broadcasted_iota(jnp.int32, h.shape, 0)
    prev = jnp.where(row == 0, carry_ref[...], prev)
    carry_ref[...] = h[tm - 1:tm, :]
    xx = prev - h
    for c in range(6):
        o_ref[c] = (h + xx * mix_ref[c:c + 1, :]).astype(o_ref.dtype)


def _rwkv_mix(x, g, mix):
    m, d = x.shape
    tm = _tile(m, 256)
    return pl.pallas_call(
        _rwkv_mix_kernel,
        grid=(m // tm,),
        in_specs=[pl.BlockSpec((tm, d), lambda i: (i, 0)),
                  pl.BlockSpec((1, d), lambda i: (0, 0)),
                  pl.BlockSpec((6, d), lambda i: (0, 0))],
        out_specs=pl.BlockSpec((6, tm, d), lambda i: (0, i, 0)),
        out_shape=jax.ShapeDtypeStruct((6, m, d), BF16),
        scratch_shapes=[pltpu.VMEM((1, d), F32)],
        compiler_params=_cparams(("arbitrary",)),
    )(x, g.reshape(1, d), mix)


def _activate(acc, act):
    if act is None:
        return acc
    if act == "tanh":
        return jnp.tanh(acc)
    if act == "sigmoid":
        return jax.nn.sigmoid(acc)
    if act == "relu2":
        return jnp.square(jnp.maximum(acc, 0.0))
    raise ValueError(act)


def _mm_kernel(*refs, nk, act, has_bias, has_res):
    x_ref, w_ref = refs[0], refs[1]
    pos = 2
    b_ref = r_ref = None
    if has_bias:
        b_ref = refs[pos]
        pos += 1
    if has_res:
        r_ref = refs[pos]
        pos += 1
    o_ref = refs[pos]

    def epilogue(acc):
        if has_bias:
            acc = acc + b_ref[...]
        acc = _activate(acc, act)
        if has_res:
            acc = acc + r_ref[...]
        o_ref[...] = acc.astype(o_ref.dtype)

    part = jnp.dot(x_ref[...], w_ref[...], preferred_element_type=F32)
    if nk == 1:
        epilogue(part)
        return
    acc_ref = refs[pos + 1]
    k = pl.program_id(2)

    @pl.when(k == 0)
    def _():
        acc_ref[...] = part

    @pl.when(k > 0)
    def _():
        acc_ref[...] += part

    @pl.when(k == nk - 1)
    def _():
        epilogue(acc_ref[...])


def _matmul(x, w, *, bias=None, act=None, residual=None, out_dtype=F32, tm=512, tn=1024, tk=4096):
    m, kd = x.shape
    n = w.shape[1]
    tm, tn, tk = _tile(m, tm), _tile(n, tn), _tile(kd, tk)
    nk = kd // tk
    in_specs = [pl.BlockSpec((tm, tk), lambda j, i, k: (i, k)),
                pl.BlockSpec((tk, tn), lambda j, i, k: (k, j))]
    args = [x, w]
    if bias is not None:
        in_specs.append(pl.BlockSpec((1, tn), lambda j, i, k: (0, j)))
        args.append(bias.reshape(1, n).astype(F32))
    if residual is not None:
        in_specs.append(pl.BlockSpec((tm, tn), lambda j, i, k: (i, j)))
        args.append(residual)
    return pl.pallas_call(
        functools.partial(_mm_kernel, nk=nk, act=act, has_bias=bias is not None,
                          has_res=residual is not None),
        grid=(n // tn, m // tm, nk),
        in_specs=in_specs,
        out_specs=pl.BlockSpec((tm, tn), lambda j, i, k: (i, j)),
        out_shape=jax.ShapeDtypeStruct((m, n), out_dtype),
        scratch_shapes=[pltpu.VMEM((tm, tn), F32)] if nk > 1 else [],
        compiler_params=_cparams(("parallel", "parallel", "arbitrary")),
    )(*args)


def _glu_mm_kernel(x_ref, wa_ref, wb_ref, ba_ref, bb_ref, o_ref):
    x = x_ref[...]
    a = jnp.dot(x, wa_ref[...], preferred_element_type=F32) + ba_ref[...]
    b = jnp.dot(x, wb_ref[...], preferred_element_type=F32) + bb_ref[...]
    o_ref[...] = (a * jax.nn.sigmoid(b)).astype(o_ref.dtype)


def _glu_matmul(x, w, bias):
    m, kd = x.shape
    n = w.shape[1] // 2
    tm, tn = _tile(m, 512), _tile(n, 512)
    nj = n // tn
    b2 = bias.reshape(1, 2 * n).astype(F32)
    return pl.pallas_call(
        _glu_mm_kernel,
        grid=(nj, m // tm),
        in_specs=[pl.BlockSpec((tm, kd), lambda j, i: (i, 0)),
                  pl.BlockSpec((kd, tn), lambda j, i: (0, j)),
                  pl.BlockSpec((kd, tn), lambda j, i: (0, j + nj)),
                  pl.BlockSpec((1, tn), lambda j, i: (0, j)),
                  pl.BlockSpec((1, tn), lambda j, i: (0, j + nj))],
        out_specs=pl.BlockSpec((tm, tn), lambda j, i: (i, j)),
        out_shape=jax.ShapeDtypeStruct((m, n), F32),
        compiler_params=_cparams(("parallel", "parallel")),
    )(x, w, w, b2, b2)


def _bdot(a, b):
    return jnp.dot(a.astype(BF16), b.astype(BF16), preferred_element_type=F32)


def _bdot_nt(a, b):
    return lax.dot_general(a.astype(BF16), b.astype(BF16), (((1,), (1,)), ((), ())),
                           preferred_element_type=F32)


def _split(a):
    hi = a.astype(BF16)
    lo = (a - hi.astype(F32)).astype(BF16)
    return hi, lo


def _dot3(a, b):
    ah, al = _split(a)
    bh, bl = _split(b)
    d = functools.partial(jnp.dot, preferred_element_type=F32)
    return d(ah, bh) + (d(ah, bl) + d(al, bh))


def _unit_lower_inverse(n_mat, eye, xr):
    n8 = jnp.where(xr < 8, n_mat, 0.0)
    n8_2 = _dot3(n8, n8)
    n8_4 = _dot3(n8_2, n8_2)
    t = eye + n8 + n8_2 + _dot3(n8, n8_2)
    t = t + _dot3(t, n8_4)
    k = 8
    while k < CHUNK:
        nk = jnp.where((xr >= k) & (xr < 2 * k), n_mat, 0.0)
        t = t + _dot3(t, _dot3(nk, t))
        k *= 2
    return t


def _wkv_kernel(*refs, pairs, has_vres):
    if has_vres:
        (r_ref, k_ref, v_ref, wl_ref, al_ref, g_ref, vf_ref, vl_ref,
         w0_ref, a0_ref, kk_ref, ka_ref, rk_ref, lg_ref, lb_ref, v0_ref, o_ref, s_ref) = refs
    else:
        (r_ref, k_ref, v_ref, wl_ref, al_ref, g_ref,
         w0_ref, a0_ref, kk_ref, ka_ref, rk_ref, lg_ref, lb_ref, o_ref, s_ref) = refs
        vf_ref = vl_ref = v0_ref = None

    @pl.when(pl.program_id(1) == 0)
    def _():
        s_ref[...] = jnp.zeros_like(s_ref)

    L = CHUNK
    row = lax.broadcasted_iota(jnp.int32, (L, L), 0)
    col = lax.broadcasted_iota(jnp.int32, (L, L), 1)
    lower = row >= col
    strict = row > col
    xr = row ^ col
    same_head = (row >= HEAD) == (col >= HEAD)
    eye = (row == col).astype(F32)
    ltri = lower.astype(BF16)
    seg = same_head.astype(BF16)

    def both(a):
        h0 = lax.broadcasted_iota(jnp.int32, a.shape, 1) % LANES < HEAD
        return jnp.concatenate([jnp.where(h0, a, 0.0), jnp.where(h0, 0.0, a)], axis=0)

    for p in range(pairs):
        sl = slice(p * LANES, (p + 1) * LANES)
        r = r_ref[:, sl]
        k = k_ref[:, sl]
        v = v_ref[:, sl]
        z = w0_ref[:, sl] + wl_ref[:, sl]
        w = jnp.minimum(z, 0.0) - jnp.log(1.0 + jnp.exp(-jnp.abs(z))) - 0.5
        lw = -jnp.exp(w)
        a = jax.nn.sigmoid(a0_ref[:, sl] + al_ref[:, sl])
        if has_vres:
            v = v + (vf_ref[:, sl] - v) * jax.nn.sigmoid(v0_ref[:, sl] + vl_ref[:, sl])
        kk = k * kk_ref[:, sl]
        kk = kk / jnp.maximum(jnp.sqrt(_bdot(kk * kk, seg)), 1e-12)
        k = k * (1.0 + (a - 1.0) * ka_ref[:, sl])
        avec = -kk
        bvec = kk * a

        lw_hi, lw_lo = _split(lw)
        cum = (jnp.dot(ltri, lw_hi, preferred_element_type=F32)
               + jnp.dot(ltri, lw_lo, preferred_element_type=F32))
        cmid = cum[L // 2 - 1:L // 2, :]
        clast = cum[L - 1:L, :]
        e_pos = jnp.exp(cum - cmid)
        e_neg = jnp.exp(cmid - cum)
        e_prev = jnp.exp(cum - lw - cmid)
        e_last = jnp.exp(clast - cum)
        p_last = jnp.exp(clast)
        p_mid = jnp.exp(cmid)
        r_h = r * e_pos
        a_h = avec * e_prev
        b_t = bvec * e_neg
        k_t = k * e_neg
        b_d = bvec * e_last
        k_d = k * e_last

        lhs = both(jnp.concatenate([a_h, r_h], axis=0))
        rhs = jnp.concatenate([b_t, k_t], axis=0)
        att = _bdot_nt(lhs, rhs)
        a_ab = [jnp.where(strict, att[2 * h * L:(2 * h + 1) * L, :L], 0.0) for h in range(2)]
        a_ak = [jnp.where(strict, att[2 * h * L:(2 * h + 1) * L, L:], 0.0) for h in range(2)]
        a_rb = [jnp.where(lower, att[(2 * h + 1) * L:(2 * h + 2) * L, :L], 0.0) for h in range(2)]
        a_rk = [jnp.where(lower, att[(2 * h + 1) * L:(2 * h + 2) * L, L:], 0.0) for h in range(2)]
        t_inv = [_unit_lower_inverse(a_ab[h], eye, xr) for h in range(2)]

        v2 = both(v)
        akv = _bdot(jnp.concatenate(a_ak, axis=1), v2)
        wu = _bdot(jnp.concatenate(t_inv, axis=1),
                   both(jnp.concatenate([a_h, akv], axis=1)))
        qy = _bdot(jnp.concatenate(a_rb, axis=1), both(wu))
        q_t = (r_h + qy[:, :LANES]) * p_mid
        y_in = qy[:, LANES:] + _bdot(jnp.concatenate(a_rk, axis=1), v2)

        bdt = b_d.T
        kdt = k_d.T
        m2 = eye * p_last + jnp.where(same_head, _bdot(bdt, wu[:, :LANES] * p_mid), 0.0)
        g2 = jnp.where(same_head,
                       _bdot(jnp.concatenate([bdt, kdt], axis=1),
                             jnp.concatenate([wu[:, LANES:], v], axis=0)), 0.0)

        s = s_ref[p]
        y = _bdot(q_t, s) + y_in
        s_ref[p] = _bdot(m2, s) + g2

        mu = _bdot(y, seg) * (1.0 / HEAD)
        d = y - mu
        var = _bdot(d * d, seg) * (1.0 / HEAD)
        yn = d * lax.rsqrt(var + GN_EPS) * lg_ref[:, sl] + lb_ref[:, sl]
        bonus = _bdot(r * k * rk_ref[:, sl], seg) * v
        o_ref[:, sl] = ((yn + bonus) * g_ref[:, sl]).astype(o_ref.dtype)


def _wkv7(r, k, v, wl, al, g, w0, a0, k_k, k_a, r_k, lnx_g, lnx_b, vres=None, pairs=2):
    t, d = r.shape
    pw = pairs * LANES
    assert t % CHUNK == 0 and d % pw == 0
    seq = pl.BlockSpec((CHUNK, pw), lambda j, c: (c, j))
    par = pl.BlockSpec((1, pw), lambda j, c: (0, j))
    row = lambda a: a.reshape(1, d).astype(F32)
    seq_args = [r, k, v, wl, al, g]
    par_args = [row(w0), row(a0), row(k_k), row(k_a), row(r_k), row(lnx_g), row(lnx_b)]
    if vres is not None:
        v_first, vl, v0 = vres
        seq_args += [v_first, vl]
        par_args += [row(v0)]
    return pl.pallas_call(
        functools.partial(_wkv_kernel, pairs=pairs, has_vres=vres is not None),
        grid=(d // pw, t // CHUNK),
        in_specs=[seq] * len(seq_args) + [par] * len(par_args),
        out_specs=seq,
        out_shape=jax.ShapeDtypeStruct((t, d), BF16),
        scratch_shapes=[pltpu.VMEM((pairs, LANES, LANES), F32)],
        compiler_params=_cparams(("parallel", "arbitrary")),
    )(*seq_args, *par_args)


def _swa_kernel(sink_ref, q_ref, kc_ref, kp_ref, vc_ref, vp_ref, o_ref, *, group):
    n = pl.program_id(0)
    W = WINDOW
    qi = lax.broadcasted_iota(jnp.int32, (2 * W, 2 * W), 0) % W
    kj = lax.broadcasted_iota(jnp.int32, (2 * W, 2 * W), 1)
    valid = (kj > qi) & (kj <= qi + W) & ((kj >= W) | (n > 0))
    top = lax.broadcasted_iota(jnp.int32, (2 * W, 1), 0) < W
    lane = lax.broadcasted_iota(jnp.int32, (W, LANES), 1)
    head0 = lane < HEAD
    zero = jnp.zeros((), BF16)
    for hk in range(SWA_KV_HEADS):
        ks = slice(hk * LANES, (hk + 1) * LANES)
        k2 = jnp.concatenate([kp_ref[:, ks], kc_ref[:, ks]], axis=0)
        v2 = jnp.concatenate([vp_ref[:, ks], vc_ref[:, ks]], axis=0)
        for i in range(group // 2):
            slab = hk * (group // 2) + i
            qs = q_ref[:, slab * LANES:(slab + 1) * LANES]
            q2 = jnp.concatenate([jnp.where(head0, qs, zero), jnp.where(head0, zero, qs)], axis=0)
            s = lax.dot_general(q2, k2, (((1,), (1,)), ((), ())), preferred_element_type=F32)
            s = jnp.where(valid, s, NEG_INF)
            sink = jnp.where(top, sink_ref[2 * slab], sink_ref[2 * slab + 1])
            m = jnp.maximum(jnp.max(s, axis=-1, keepdims=True), sink)
            e = jnp.exp(s - m)
            den = jnp.sum(e, axis=-1, keepdims=True) + jnp.exp(sink - m)
            pr = (e / den).astype(BF16)
            o = jnp.dot(pr, v2, preferred_element_type=F32)
            o_ref[:, slab * LANES:(slab + 1) * LANES] = jnp.where(head0, o[:W], o[W:]).astype(o_ref.dtype)


def _swa_attention(qkv, sinks, d):
    t = qkv.shape[0]
    kvw = SWA_KV_HEADS * LANES
    qb = d // kvw
    group = d // HEAD // SWA_KV_HEADS
    assert group % 2 == 0 and d % kvw == 0 and t % WINDOW == 0
    prev = lambda n: jnp.maximum(n - 1, 0)
    return pl.pallas_call(
        functools.partial(_swa_kernel, group=group),
        grid=(t // WINDOW,),
        in_specs=[pl.BlockSpec(memory_space=pltpu.SMEM),
                  pl.BlockSpec((WINDOW, d), lambda n: (n, 0)),
                  pl.BlockSpec((WINDOW, kvw), lambda n: (n, qb)),
                  pl.BlockSpec((WINDOW, kvw), lambda n: (prev(n), qb)),
                  pl.BlockSpec((WINDOW, kvw), lambda n: (n, qb + 1)),
                  pl.BlockSpec((WINDOW, kvw), lambda n: (prev(n), qb + 1))],
        out_specs=pl.BlockSpec((WINDOW, d), lambda n: (n, 0)),
        out_shape=jax.ShapeDtypeStruct((t, d), BF16),
        compiler_params=_cparams(("arbitrary",)),
    )(sinks.astype(F32), qkv, qkv, qkv, qkv, qkv)


def _conv_kernel(u_ref, halo_ref, w_ref, b_ref, g_ref, beta_ref, o_ref, buf_ref, acc_ref):
    tm, d = u_ref.shape
    first = pl.program_id(0) == 0
    buf_ref[0:CONV_HALO, :] = jnp.where(first, 0.0, halo_ref[...])
    buf_ref[CONV_HALO:, :] = u_ref[...]
    rb = min(tm, 64)
    base = CONV_HALO - (CONV_WIDTH - 1)

    def col_body(c, carry):
        cs = pl.ds(pl.multiple_of(c * LANES, LANES), LANES)
        for r0 in range(0, tm, rb):
            acc = jnp.zeros((rb, LANES), F32)
            for j in range(CONV_WIDTH):
                acc = acc + buf_ref[pl.ds(r0 + base + j, rb), cs] * w_ref[pl.ds(j, 1), cs]
            acc_ref[pl.ds(r0, rb), cs] = acc
        return carry

    lax.fori_loop(0, d // LANES, col_body, 0)
    y = acc_ref[...] + b_ref[...]
    mu = jnp.mean(y, axis=-1, keepdims=True)
    yc = y - mu
    var = jnp.mean(yc * yc, axis=-1, keepdims=True)
    yn = yc * lax.rsqrt(var + LN_EPS) * g_ref[...] + beta_ref[...]
    o_ref[...] = (yn * jax.nn.sigmoid(yn)).astype(o_ref.dtype)


def _conv_ln_silu(u, w_dw, b_dw, ln_g, ln_b):
    t, d = u.shape
    tm = _tile(t, 256)
    assert tm % CONV_HALO == 0
    ratio = tm // CONV_HALO
    row = lambda a: a.reshape(1, d).astype(F32)
    vec = pl.BlockSpec((1, d), lambda i: (0, 0))
    return pl.pallas_call(
        _conv_kernel,
        grid=(t // tm,),
        in_specs=[pl.BlockSpec((tm, d), lambda i: (i, 0)),
                  pl.BlockSpec((CONV_HALO, d), lambda i: (jnp.maximum(i * ratio - 1, 0), 0)),
                  pl.BlockSpec((CONV_WIDTH, d), lambda i: (0, 0)),
                  vec, vec, vec],
        out_specs=pl.BlockSpec((tm, d), lambda i: (i, 0)),
        out_shape=jax.ShapeDtypeStruct((t, d), BF16),
        scratch_shapes=[pltpu.VMEM((tm + CONV_HALO, d), F32), pltpu.VMEM((tm, d), F32)],
        compiler_params=_cparams(("arbitrary",)),
    )(u, u, w_dw.astype(F32), row(b_dw), row(ln_g), row(ln_b))


def _pad_to(a, axis, mult):
    pad = (-a.shape[axis]) % mult
    if pad == 0:
        return a
    widths = [(0, 0)] * a.ndim
    widths[axis] = (0, pad)
    return jnp.pad(a, widths)


def _lora(xs, w1, w2, act):
    w1p = _pad_to(w1, 1, LANES).astype(BF16)
    w2p = _pad_to(w2, 0, LANES).astype(BF16)
    mid = _matmul(xs, w1p, act=act, out_dtype=BF16)
    return _matmul(mid, w2p)


def _rwkv_layer(x, norm_g, v_first, vres, mix, w_rkv, w0, w1, w2, a0, a1, a2, g1, g2,
                k_k, k_a, r_k, lnx_g, lnx_b, w_o):
    d = x.shape[1]
    xs = _rwkv_mix(x, norm_g, mix)
    r = _matmul(xs[0], w_rkv[0].astype(BF16))
    k = _matmul(xs[1], w_rkv[1].astype(BF16))
    v = _matmul(xs[2], w_rkv[2].astype(BF16))
    wl = _lora(xs[3], w1, w2, "tanh")
    al = _lora(xs[4], a1, a2, None)
    g = _lora(xs[5], g1, g2, "sigmoid")
    if vres is None:
        v_first = v
        vres_args = None
    else:
        v0, v1, v2 = vres
        vres_args = (v_first, _lora(xs[2], v1, v2, None), v0)
    z = _wkv7(r, k, v, wl, al, g, w0, a0, k_k, k_a, r_k.reshape(d), lnx_g, lnx_b, vres=vres_args)
    return _matmul(z, w_o.astype(BF16), residual=x), v_first


def _swa_layer(x, norm_g, w_qkv, b_qkv, sinks, w_o, b_o):
    d = x.shape[1]
    kvw = SWA_KV_HEADS * HEAD
    scale = HEAD ** -0.5

    def dup(a):
        a = a.reshape(a.shape[:-1] + (SWA_KV_HEADS, 1, HEAD))
        a = jnp.broadcast_to(a, a.shape[:-2] + (2, HEAD))
        return a.reshape(a.shape[:-3] + (2 * kvw,))

    w_ext = jnp.concatenate([w_qkv[:, :d] * scale, dup(w_qkv[:, d:d + kvw]), dup(w_qkv[:, d + kvw:])], axis=1)
    b_ext = jnp.concatenate([b_qkv[:d] * scale, dup(b_qkv[d:d + kvw]), dup(b_qkv[d + kvw:])], axis=0)
    h = _rmsnorm(x, norm_g, BF16)
    qkv = _matmul(h, w_ext.astype(BF16), bias=b_ext, out_dtype=BF16)
    o = _swa_attention(qkv, sinks, d)
    return _matmul(o, w_o.astype(BF16), bias=b_o, residual=x)


def _conv_layer(x, norm_g, w_pw1, b_pw1, w_dw, b_dw, ln_g, ln_b, w_pw2, b_pw2):
    h = _rmsnorm(x, norm_g, BF16)
    u = _glu_matmul(h, w_pw1.astype(BF16), b_pw1)
    u = _conv_ln_silu(u, w_dw, b_dw, ln_g, ln_b)
    return _matmul(u, w_pw2.astype(BF16), bias=b_pw2, residual=x)


def _mlp(x, norm_g, w_in, w_out):
    h = _rmsnorm(x, norm_g, BF16)
    a = _matmul(h, w_in.astype(BF16), act="relu2", out_dtype=BF16)
    return _matmul(a, w_out.astype(BF16), residual=x, tm=1024, tn=1024, tk=2048)


def kernel(x, norm_mix_g, norm_mlp_g, norm_f_g, rwkv_mix, rwkv_w_rkv, rwkv_w0, rwkv_w1, rwkv_w2, rwkv_a0, rwkv_a1, rwkv_a2, rwkv_v0, rwkv_v1, rwkv_v2, rwkv_g1, rwkv_g2, rwkv_k_k, rwkv_k_a, rwkv_r_k, rwkv_lnx_g, rwkv_lnx_b, rwkv_w_o, swa_w_qkv, swa_b_qkv, swa_sinks, swa_w_o, swa_b_o, conv_w_pw1, conv_b_pw1, conv_w_dw, conv_b_dw, conv_ln_g, conv_ln_b, conv_w_pw2, conv_b_pw2, mlp_w_in, mlp_w_out):
    b, t, d = x.shape
    assert b == 1, "token shift and the WKV scan are written for a single sequence"
    depth = norm_mix_g.shape[0]
    xs = x.reshape(t, d)
    ia = ib = ic = 0
    v_first = None
    for i in range(depth):
        kind = i % 3
        if kind == 0:
            vres = None if ia == 0 else (rwkv_v0[ia - 1], rwkv_v1[ia - 1], rwkv_v2[ia - 1])
            xs, v_first = _rwkv_layer(xs, norm_mix_g[i], v_first, vres, rwkv_mix[ia], rwkv_w_rkv[ia],
                                      rwkv_w0[ia], rwkv_w1[ia], rwkv_w2[ia], rwkv_a0[ia], rwkv_a1[ia],
                                      rwkv_a2[ia], rwkv_g1[ia], rwkv_g2[ia], rwkv_k_k[ia], rwkv_k_a[ia],
                                      rwkv_r_k[ia], rwkv_lnx_g[ia], rwkv_lnx_b[ia], rwkv_w_o[ia])
            ia += 1
        elif kind == 1:
            xs = _swa_layer(xs, norm_mix_g[i], swa_w_qkv[ib], swa_b_qkv[ib], swa_sinks[ib],
                            swa_w_o[ib], swa_b_o[ib])
            ib += 1
        else:
            xs = _conv_layer(xs, norm_mix_g[i], conv_w_pw1[ic], conv_b_pw1[ic], conv_w_dw[ic],
                             conv_b_dw[ic], conv_ln_g[ic], conv_ln_b[ic], conv_w_pw2[ic], conv_b_pw2[ic])
            ic += 1
        xs = _mlp(xs, norm_mlp_g[i], mlp_w_in[i], mlp_w_out[i])
    return _rmsnorm(xs, norm_f_g, F32).reshape(b, t, d)
```

```python
import functools

import jax
import jax.numpy as jnp
from jax import lax
from jax.experimental import pallas as pl
from jax.experimental.pallas import tpu as pltpu

F32 = jnp.float32
BF16 = jnp.bfloat16

LANES = 128
HEAD = 64
CHUNK = 128
WKV_PAIRS = 8
WINDOW = 128
SWA_KV_HEADS = 8
CONV_WIDTH = 31
CONV_HALO = 32
RMS_EPS = 1e-5
LN_EPS = 1e-5
GN_EPS = 64e-5
NEG_INF = -1e30
VMEM_LIMIT = 48 * 1024 * 1024


def _cparams(sem):
    return pltpu.CompilerParams(dimension_semantics=sem, vmem_limit_bytes=VMEM_LIMIT)


def _tile(n, pref):
    if n <= pref:
        return n
    t = pref
    while n % t:
        t //= 2
    return t


def _rmsnorm_kernel(x_ref, g_ref, o_ref):
    x = x_ref[...]
    y = x * lax.rsqrt(jnp.mean(x * x, axis=-1, keepdims=True) + RMS_EPS)
    o_ref[...] = (y * g_ref[...]).astype(o_ref.dtype)


def _rmsnorm(x, g, out_dtype):
    m, d = x.shape
    tm = _tile(m, 256)
    return pl.pallas_call(
        _rmsnorm_kernel,
        grid=(m // tm,),
        in_specs=[pl.BlockSpec((tm, d), lambda i: (i, 0)),
                  pl.BlockSpec((1, d), lambda i: (0, 0))],
        out_specs=pl.BlockSpec((tm, d), lambda i: (i, 0)),
        out_shape=jax.ShapeDtypeStruct((m, d), out_dtype),
        compiler_params=_cparams(("arbitrary",)),
        name="rmsnorm",
    )(x, g.reshape(1, d))


def _rwkv_mix_kernel(x_ref, g_ref, mix_ref, o_ref, carry_ref):
    @pl.when(pl.program_id(0) == 0)
    def _():
        carry_ref[...] = jnp.zeros_like(carry_ref)

    x = x_ref[...]
    h = x * lax.rsqrt(jnp.mean(x * x, axis=-1, keepdims=True) + RMS_EPS) * g_ref[...]
    tm = h.shape[0]
    prev = pltpu.roll(h, shift=1, axis=0)
    row = lax.broadcasted_iota(jnp.int32, h.shape, 0)
    prev = jnp.where(row == 0, carry_ref[...], prev)
    carry_ref[...] = h[tm - 1:tm, :]
    xx = prev - h
    for c in range(6):
        o_ref[c] = (h + xx * mix_ref[c:c + 1, :]).astype(o_ref.dtype)


def _rwkv_mix(x, g, mix):
    m, d = x.shape
    tm = _tile(m, 256)
    return pl.pallas_call(
        _rwkv_mix_kernel,
        grid=(m // tm,),
        in_specs=[pl.BlockSpec((tm, d), lambda i: (i, 0)),
                  pl.BlockSpec((1, d), lambda i: (0, 0)),
                  pl.BlockSpec((6, d), lambda i: (0, 0))],
        out_specs=pl.BlockSpec((6, tm, d), lambda i: (0, i, 0)),
        out_shape=jax.ShapeDtypeStruct((6, m, d), BF16),
        scratch_shapes=[pltpu.VMEM((1, d), F32)],
        compiler_params=_cparams(("arbitrary",)),
        name="rwkv_mix",
    )(x, g.reshape(1, d), mix)


def _activate(acc, act):
    if act is None:
        return acc
    if act == "tanh":
        return jnp.tanh(acc)
    if act == "sigmoid":
        return jax.nn.sigmoid(acc)
    if act == "relu2":
        return jnp.square(jnp.maximum(acc, 0.0))
    raise ValueError(act)


def _mm_kernel(*refs, nk, act, has_bias, has_res, cast_w):
    x_ref, w_ref = refs[0], refs[1]
    pos = 2
    b_ref = r_ref = None
    if has_bias:
        b_ref = refs[pos]
        pos += 1
    if has_res:
        r_ref = refs[pos]
        pos += 1
    o_ref = refs[pos]
    scratch = list(refs[pos + 1:])

    def epilogue(acc):
        if has_bias:
            acc = acc + b_ref[...]
        acc = _activate(acc, act)
        if has_res:
            acc = acc + r_ref[...]
        o_ref[...] = acc.astype(o_ref.dtype)

    if cast_w:
        wb_ref = scratch.pop(0)

        @pl.when(pl.program_id(1) == 0)
        def _():
            wb_ref[...] = w_ref[...].astype(BF16)

        w = wb_ref[...]
    else:
        w = w_ref[...]
    part = jnp.dot(x_ref[...], w, preferred_element_type=F32)
    if nk == 1:
        epilogue(part)
        return
    acc_ref = scratch.pop(0)
    k = pl.program_id(2)

    @pl.when(k == 0)
    def _():
        acc_ref[...] = part

    @pl.when(k > 0)
    def _():
        acc_ref[...] += part

    @pl.when(k == nk - 1)
    def _():
        epilogue(acc_ref[...])


def _matmul(x, w, *, name, x_index=None, w_index=None, bias=None, act=None, residual=None,
            out_dtype=F32, tm=512, tn=None, tk=4096):
    m, kd = x.shape[-2:]
    n = w.shape[-1]
    cast_w = w.dtype != BF16
    if tn is None:
        tn = 512 if cast_w else 1024
    tm, tn, tk = _tile(m, tm), _tile(n, tn), _tile(kd, tk)
    nk = kd // tk
    assert not cast_w or nk == 1, "the cast-once path needs the weight block fixed across row tiles"
    if w.ndim == 3:
        w_spec = pl.BlockSpec((None, tk, tn), lambda j, i, k: (w_index, k, j))
    else:
        w_spec = pl.BlockSpec((tk, tn), lambda j, i, k: (k, j))
    if x.ndim == 3:
        x_spec = pl.BlockSpec((None, tm, tk), lambda j, i, k: (x_index, i, k))
    else:
        x_spec = pl.BlockSpec((tm, tk), lambda j, i, k: (i, k))
    in_specs = [x_spec, w_spec]
    args = [x, w]
    if bias is not None:
        in_specs.append(pl.BlockSpec((1, tn), lambda j, i, k: (0, j)))
        args.append(bias.reshape(1, n).astype(F32))
    if residual is not None:
        in_specs.append(pl.BlockSpec((tm, tn), lambda j, i, k: (i, j)))
        args.append(residual)
    scratch = []
    if cast_w:
        scratch.append(pltpu.VMEM((tk, tn), BF16))
    if nk > 1:
        scratch.append(pltpu.VMEM((tm, tn), F32))
    return pl.pallas_call(
        functools.partial(_mm_kernel, nk=nk, act=act, has_bias=bias is not None,
                          has_res=residual is not None, cast_w=cast_w),
        grid=(n // tn, m // tm, nk),
        in_specs=in_specs,
        out_specs=pl.BlockSpec((tm, tn), lambda j, i, k: (i, j)),
        out_shape=jax.ShapeDtypeStruct((m, n), out_dtype),
        scratch_shapes=scratch,
        compiler_params=_cparams(("arbitrary" if cast_w else "parallel",) * 2 + ("arbitrary",)),
        name=name,
    )(*args)


def _glu_mm_kernel(x_ref, wa_ref, wb_ref, ba_ref, bb_ref, o_ref, wab_ref, wbb_ref):
    @pl.when(pl.program_id(1) == 0)
    def _():
        wab_ref[...] = wa_ref[...].astype(BF16)
        wbb_ref[...] = wb_ref[...].astype(BF16)

    x = x_ref[...]
    a = jnp.dot(x, wab_ref[...], preferred_element_type=F32) + ba_ref[...]
    b = jnp.dot(x, wbb_ref[...], preferred_element_type=F32) + bb_ref[...]
    o_ref[...] = (a * jax.nn.sigmoid(b)).astype(o_ref.dtype)


def _glu_matmul(x, w, w_index, bias):
    m, kd = x.shape
    n = w.shape[-1] // 2
    tm, tn = _tile(m, 512), _tile(n, 256)
    nj = n // tn
    b2 = bias.reshape(1, 2 * n).astype(F32)
    return pl.pallas_call(
        _glu_mm_kernel,
        grid=(nj, m // tm),
        in_specs=[pl.BlockSpec((tm, kd), lambda j, i: (i, 0)),
                  pl.BlockSpec((None, kd, tn), lambda j, i: (w_index, 0, j)),
                  pl.BlockSpec((None, kd, tn), lambda j, i: (w_index, 0, j + nj)),
                  pl.BlockSpec((1, tn), lambda j, i: (0, j)),
                  pl.BlockSpec((1, tn), lambda j, i: (0, j + nj))],
        out_specs=pl.BlockSpec((tm, tn), lambda j, i: (i, j)),
        out_shape=jax.ShapeDtypeStruct((m, n), F32),
        scratch_shapes=[pltpu.VMEM((kd, tn), BF16), pltpu.VMEM((kd, tn), BF16)],
        compiler_params=_cparams(("arbitrary", "arbitrary")),
        name="conv_pw1_glu",
    )(x, w, w, b2, b2)


def _bdot(a, b):
    return jnp.dot(a.astype(BF16), b.astype(BF16), preferred_element_type=F32)


def _bdot_nt(a, b):
    return lax.dot_general(a.astype(BF16), b.astype(BF16), (((1,), (1,)), ((), ())),
                           preferred_element_type=F32)


def _split(a):
    hi = a.astype(BF16)
    lo = (a - hi.astype(F32)).astype(BF16)
    return hi, lo


def _blockdiag(y):
    left = lax.broadcasted_iota(jnp.int32, y.shape, 1) < y.shape[0]
    return jnp.concatenate([jnp.where(left, y, 0.0), jnp.where(left, 0.0, y)], axis=0)


def _hdot(x, y):
    return _bdot(x, _blockdiag(y))


def _unit_lower_inverses(n_list, eye2, xr2):
    n8 = [jnp.where(xr2 < 8, n, 0.0) for n in n_list]
    n8_2 = [_hdot(x, x) for x in n8]
    n8_4 = [_hdot(x, x) for x in n8_2]
    n8_3 = [_hdot(x, y) for x, y in zip(n8, n8_2)]
    t = [eye2 + a + b + c for a, b, c in zip(n8, n8_2, n8_3)]
    t = [x + _hdot(x, y) for x, y in zip(t, n8_4)]
    k = 8
    while k < CHUNK:
        level = (xr2 >= k) & (xr2 < 2 * k)
        u = [_hdot(jnp.where(level, n, 0.0), x) for n, x in zip(n_list, t)]
        t = [x + _hdot(x, y) for x, y in zip(t, u)]
        k *= 2
    return t


def _wkv_kernel(*refs, pairs, has_vres):
    if has_vres:
        (r_ref, k_ref, v_ref, wl_ref, al_ref, g_ref, vf_ref, vl_ref,
         w0_ref, a0_ref, kk_ref, ka_ref, rk_ref, lg_ref, lb_ref, v0_ref, o_ref, s_ref) = refs
    else:
        (r_ref, k_ref, v_ref, wl_ref, al_ref, g_ref,
         w0_ref, a0_ref, kk_ref, ka_ref, rk_ref, lg_ref, lb_ref, o_ref, s_ref) = refs
        vf_ref = vl_ref = v0_ref = None

    @pl.when(pl.program_id(1) == 0)
    def _():
        s_ref[...] = jnp.zeros_like(s_ref)

    L = CHUNK
    row = lax.broadcasted_iota(jnp.int32, (L, L), 0)
    col = lax.broadcasted_iota(jnp.int32, (L, L), 1)
    lower = row >= col
    same_head = (row >= HEAD) == (col >= HEAD)
    eye = (row == col).astype(F32)
    ltri = lower.astype(BF16)
    seg = same_head.astype(BF16)
    row2 = lax.broadcasted_iota(jnp.int32, (L, 2 * L), 0)
    col2 = lax.broadcasted_iota(jnp.int32, (L, 2 * L), 1) % L
    lower2 = row2 >= col2
    strict2 = row2 > col2
    xr2 = row2 ^ col2
    eye2 = (row2 == col2).astype(F32)

    def both(a):
        h0 = lax.broadcasted_iota(jnp.int32, a.shape, 1) % LANES < HEAD
        return jnp.concatenate([jnp.where(h0, a, 0.0), jnp.where(h0, 0.0, a)], axis=0)

    slabs = [slice(p * LANES, (p + 1) * LANES) for p in range(pairs)]

    def segsum(x):
        return jnp.concatenate([_bdot(x[:, sl], seg) for sl in slabs], axis=1)

    r = r_ref[...]
    k = k_ref[...]
    v = v_ref[...]
    z = w0_ref[...] + wl_ref[...]
    w = jnp.minimum(z, 0.0) - jnp.log(1.0 + jnp.exp(-jnp.abs(z))) - 0.5
    lw = -jnp.exp(w)
    a = jax.nn.sigmoid(a0_ref[...] + al_ref[...])
    if has_vres:
        v = v + (vf_ref[...] - v) * jax.nn.sigmoid(v0_ref[...] + vl_ref[...])
    kk = k * kk_ref[...]
    kk = kk / jnp.maximum(jnp.sqrt(segsum(kk * kk)), 1e-12)
    k = k * (1.0 + (a - 1.0) * ka_ref[...])
    avec = -kk
    bvec = kk * a

    lw_hi, lw_lo = _split(lw)
    cum = (jnp.dot(ltri, lw_hi, preferred_element_type=F32)
           + jnp.dot(ltri, lw_lo, preferred_element_type=F32))
    cmid = cum[L // 2 - 1:L // 2, :]
    clast = cum[L - 1:L, :]
    e_pos = jnp.exp(cum - cmid)
    e_neg = jnp.exp(cmid - cum)
    e_prev = jnp.exp(cum - lw - cmid)
    e_last = jnp.exp(clast - cum)
    p_last = jnp.exp(clast)
    p_mid = jnp.exp(cmid)
    r_h = r * e_pos
    a_h = avec * e_prev
    b_t = bvec * e_neg
    k_t = k * e_neg
    b_d = bvec * e_last
    k_d = k * e_last

    def sbs(att, r0, c0):
        return jnp.concatenate([att[r0:r0 + L, c0:c0 + L], att[r0 + 2 * L:r0 + 3 * L, c0:c0 + L]], axis=1)

    att = [_bdot_nt(both(jnp.concatenate([a_h[:, sl], r_h[:, sl]], axis=0)),
                    jnp.concatenate([b_t[:, sl], k_t[:, sl]], axis=0))
           for sl in slabs]
    a_ab = [jnp.where(strict2, sbs(x, 0, 0), 0.0) for x in att]
    a_ak = [jnp.where(strict2, sbs(x, 0, L), 0.0) for x in att]
    a_rb = [jnp.where(lower2, sbs(x, L, 0), 0.0) for x in att]
    a_rk = [jnp.where(lower2, sbs(x, L, L), 0.0) for x in att]
    t_inv = _unit_lower_inverses(a_ab, eye2, xr2)

    v2 = [both(v[:, sl]) for sl in slabs]
    akv = [_bdot(x, y) for x, y in zip(a_ak, v2)]
    wu = [_bdot(t, both(jnp.concatenate([a_h[:, sl], u], axis=1)))
          for t, u, sl in zip(t_inv, akv, slabs)]
    qy = [_bdot(x, both(y)) for x, y in zip(a_rb, wu)]
    yk = [_bdot(x, y) for x, y in zip(a_rk, v2)]

    y_parts = []
    for p, sl in enumerate(slabs):
        q_t = (r_h[:, sl] + qy[p][:, :LANES]) * p_mid[:, sl]
        y_in = qy[p][:, LANES:] + yk[p]
        bdt = b_d[:, sl].T
        kdt = k_d[:, sl].T
        m2 = eye * p_last[:, sl] + jnp.where(
            same_head, _bdot(bdt, wu[p][:, :LANES] * p_mid[:, sl]), 0.0)
        g2 = jnp.where(same_head,
                       _bdot(jnp.concatenate([bdt, kdt], axis=1),
                             jnp.concatenate([wu[p][:, LANES:], v[:, sl]], axis=0)), 0.0)
        s = s_ref[p]
        y_parts.append(_bdot(q_t, s) + y_in)
        s_ref[p] = _bdot(m2, s) + g2
    y = jnp.concatenate(y_parts, axis=1)

    mu = segsum(y) * (1.0 / HEAD)
    d = y - mu
    var = segsum(d * d) * (1.0 / HEAD)
    yn = d * lax.rsqrt(var + GN_EPS) * lg_ref[...] + lb_ref[...]
    bonus = segsum(r * k * rk_ref[...]) * v
    o_ref[...] = ((yn + bonus) * g_ref[...]).astype(o_ref.dtype)


def _wkv7(r, k, v, wl, al, g, w0, a0, k_k, k_a, r_k, lnx_g, lnx_b, vres=None):
    t, d = r.shape
    pairs = min(WKV_PAIRS, d // LANES)
    pw = pairs * LANES
    assert t % CHUNK == 0 and d % pw == 0
    seq = pl.BlockSpec((CHUNK, pw), lambda j, c: (c, j))
    par = pl.BlockSpec((1, pw), lambda j, c: (0, j))
    row = lambda a: a.reshape(1, d).astype(F32)
    seq_args = [r, k, v, wl, al, g]
    par_args = [row(w0), row(a0), row(k_k), row(k_a), row(r_k), row(lnx_g), row(lnx_b)]
    if vres is not None:
        v_first, vl, v0 = vres
        seq_args += [v_first, vl]
        par_args += [row(v0)]
    return pl.pallas_call(
        functools.partial(_wkv_kernel, pairs=pairs, has_vres=vres is not None),
        grid=(d // pw, t // CHUNK),
        in_specs=[seq] * len(seq_args) + [par] * len(par_args),
        out_specs=seq,
        out_shape=jax.ShapeDtypeStruct((t, d), BF16),
        scratch_shapes=[pltpu.VMEM((pairs, LANES, LANES), F32)],
        compiler_params=_cparams(("parallel", "arbitrary")),
        name="wkv7",
    )(*seq_args, *par_args)


def _swa_kernel(sink_ref, q_ref, kc_ref, kp_ref, vc_ref, vp_ref, o_ref, *, group):
    n = pl.program_id(0)
    W = WINDOW
    qi = lax.broadcasted_iota(jnp.int32, (2 * W, 2 * W), 0) % W
    kj = lax.broadcasted_iota(jnp.int32, (2 * W, 2 * W), 1)
    valid = (kj > qi) & (kj <= qi + W) & ((kj >= W) | (n > 0))
    top = lax.broadcasted_iota(jnp.int32, (2 * W, 1), 0) < W
    lane = lax.broadcasted_iota(jnp.int32, (W, LANES), 1)
    head0 = lane < HEAD
    zero = jnp.zeros((), BF16)
    for hk in range(SWA_KV_HEADS):
        ks = slice(hk * LANES, (hk + 1) * LANES)
        k2 = jnp.concatenate([kp_ref[:, ks], kc_ref[:, ks]], axis=0)
        v2 = jnp.concatenate([vp_ref[:, ks], vc_ref[:, ks]], axis=0)
        for i in range(group // 2):
            slab = hk * (group // 2) + i
            qs = q_ref[:, slab * LANES:(slab + 1) * LANES]
            q2 = jnp.concatenate([jnp.where(head0, qs, zero), jnp.where(head0, zero, qs)], axis=0)
            s = lax.dot_general(q2, k2, (((1,), (1,)), ((), ())), preferred_element_type=F32)
            s = jnp.where(valid, s, NEG_INF)
            sink = jnp.where(top, sink_ref[2 * slab], sink_ref[2 * slab + 1])
            m = jnp.maximum(jnp.max(s, axis=-1, keepdims=True), sink)
            e = jnp.exp(s - m)
            den = jnp.sum(e, axis=-1, keepdims=True) + jnp.exp(sink - m)
            pr = (e / den).astype(BF16)
            o = jnp.dot(pr, v2, preferred_element_type=F32)
            o_ref[:, slab * LANES:(slab + 1) * LANES] = jnp.where(head0, o[:W], o[W:]).astype(o_ref.dtype)


def _swa_attention(qkv, sinks, d):
    t = qkv.shape[0]
    kvw = SWA_KV_HEADS * LANES
    qb = d // kvw
    group = d // HEAD // SWA_KV_HEADS
    assert group % 2 == 0 and d % kvw == 0 and t % WINDOW == 0
    prev = lambda n: jnp.maximum(n - 1, 0)
    return pl.pallas_call(
        functools.partial(_swa_kernel, group=group),
        grid=(t // WINDOW,),
        in_specs=[pl.BlockSpec(memory_space=pltpu.SMEM),
                  pl.BlockSpec((WINDOW, d), lambda n: (n, 0)),
                  pl.BlockSpec((WINDOW, kvw), lambda n: (n, qb)),
                  pl.BlockSpec((WINDOW, kvw), lambda n: (prev(n), qb)),
                  pl.BlockSpec((WINDOW, kvw), lambda n: (n, qb + 1)),
                  pl.BlockSpec((WINDOW, kvw), lambda n: (prev(n), qb + 1))],
        out_specs=pl.BlockSpec((WINDOW, d), lambda n: (n, 0)),
        out_shape=jax.ShapeDtypeStruct((t, d), BF16),
        compiler_params=_cparams(("arbitrary",)),
        name="swa_attention",
    )(sinks.astype(F32), qkv, qkv, qkv, qkv, qkv)


def _conv_kernel(u_ref, halo_ref, w_ref, b_ref, g_ref, beta_ref, o_ref, buf_ref, acc_ref):
    tm, d = u_ref.shape
    first = pl.program_id(0) == 0
    buf_ref[0:CONV_HALO, :] = jnp.where(first, 0.0, halo_ref[...])
    buf_ref[CONV_HALO:, :] = u_ref[...]
    rb = min(tm, 64)
    base = CONV_HALO - (CONV_WIDTH - 1)

    def col_body(c, carry):
        cs = pl.ds(pl.multiple_of(c * LANES, LANES), LANES)
        for r0 in range(0, tm, rb):
            acc = jnp.zeros((rb, LANES), F32)
            for j in range(CONV_WIDTH):
                acc = acc + buf_ref[pl.ds(r0 + base + j, rb), cs] * w_ref[pl.ds(j, 1), cs]
            acc_ref[pl.ds(r0, rb), cs] = acc
        return carry

    lax.fori_loop(0, d // LANES, col_body, 0)
    y = acc_ref[...] + b_ref[...]
    mu = jnp.mean(y, axis=-1, keepdims=True)
    yc = y - mu
    var = jnp.mean(yc * yc, axis=-1, keepdims=True)
    yn = yc * lax.rsqrt(var + LN_EPS) * g_ref[...] + beta_ref[...]
    o_ref[...] = (yn * jax.nn.sigmoid(yn)).astype(o_ref.dtype)


def _conv_ln_silu(u, w_dw, b_dw, ln_g, ln_b):
    t, d = u.shape
    tm = _tile(t, 256)
    assert tm % CONV_HALO == 0
    ratio = tm // CONV_HALO
    row = lambda a: a.reshape(1, d).astype(F32)
    vec = pl.BlockSpec((1, d), lambda i: (0, 0))
    return pl.pallas_call(
        _conv_kernel,
        grid=(t // tm,),
        in_specs=[pl.BlockSpec((tm, d), lambda i: (i, 0)),
                  pl.BlockSpec((CONV_HALO, d), lambda i: (jnp.maximum(i * ratio - 1, 0), 0)),
                  pl.BlockSpec((CONV_WIDTH, d), lambda i: (0, 0)),
                  vec, vec, vec],
        out_specs=pl.BlockSpec((tm, d), lambda i: (i, 0)),
        out_shape=jax.ShapeDtypeStruct((t, d), BF16),
        scratch_shapes=[pltpu.VMEM((tm + CONV_HALO, d), F32), pltpu.VMEM((tm, d), F32)],
        compiler_params=_cparams(("arbitrary",)),
        name="conv_dw_ln_silu",
    )(u, u, w_dw.astype(F32), row(b_dw), row(ln_g), row(ln_b))


def _pad_to(a, axis, mult):
    pad = (-a.shape[axis]) % mult
    if pad == 0:
        return a
    widths = [(0, 0)] * a.ndim
    widths[axis] = (0, pad)
    return jnp.pad(a, widths)


def _lora(xs, stream, w1, w2, act, name):
    w1p = _pad_to(w1, 1, LANES).astype(BF16)
    w2p = _pad_to(w2, 0, LANES).astype(BF16)
    mid = _matmul(xs, w1p, x_index=stream, act=act, out_dtype=BF16, name=name + "_down")
    return _matmul(mid, w2p, name=name + "_up")


def _rwkv_layer(x, norm_g, v_first, vres, mix, w_rkv, layer, w0, w1, w2, a0, a1, a2, g1, g2,
                k_k, k_a, r_k, lnx_g, lnx_b, w_o):
    d = x.shape[1]
    xs = _rwkv_mix(x, norm_g, mix)
    r, k, v = (_matmul(xs, w_rkv, x_index=c, w_index=3 * layer + c, name="rwkv_" + "rkv"[c])
               for c in range(3))
    wl = _lora(xs, 3, w1, w2, "tanh", "rwkv_w_lora")
    al = _lora(xs, 4, a1, a2, None, "rwkv_a_lora")
    g = _lora(xs, 5, g1, g2, "sigmoid", "rwkv_g_lora")
    if vres is None:
        v_first = v
        vres_args = None
    else:
        v0, v1, v2 = vres
        vres_args = (v_first, _lora(xs, 2, v1, v2, None, "rwkv_v_lora"), v0)
    z = _wkv7(r, k, v, wl, al, g, w0, a0, k_k, k_a, r_k.reshape(d), lnx_g, lnx_b, vres=vres_args)
    return _matmul(z, w_o, w_index=layer, residual=x, name="rwkv_out"), v_first


def _swa_layer(x, norm_g, w_qkv, b_qkv, sinks, w_o, layer, b_o):
    d = x.shape[1]
    kvw = SWA_KV_HEADS * HEAD
    scale = HEAD ** -0.5

    def dup(a):
        a = a.reshape(a.shape[:-1] + (SWA_KV_HEADS, 1, HEAD))
        a = jnp.broadcast_to(a, a.shape[:-2] + (2, HEAD))
        return a.reshape(a.shape[:-3] + (2 * kvw,))

    w_ext = jnp.concatenate([w_qkv[:, :d] * scale, dup(w_qkv[:, d:d + kvw]), dup(w_qkv[:, d + kvw:])], axis=1)
    b_ext = jnp.concatenate([b_qkv[:d] * scale, dup(b_qkv[d:d + kvw]), dup(b_qkv[d + kvw:])], axis=0)
    h = _rmsnorm(x, norm_g, BF16)
    qkv = _matmul(h, w_ext.astype(BF16), bias=b_ext, out_dtype=BF16, name="swa_qkv")
    o = _swa_attention(qkv, sinks, d)
    return _matmul(o, w_o, w_index=layer, bias=b_o, residual=x, name="swa_out")


def _conv_layer(x, norm_g, w_pw1, layer, b_pw1, w_dw, b_dw, ln_g, ln_b, w_pw2, b_pw2):
    h = _rmsnorm(x, norm_g, BF16)
    u = _glu_matmul(h, w_pw1, layer, b_pw1)
    u = _conv_ln_silu(u, w_dw, b_dw, ln_g, ln_b)
    return _matmul(u, w_pw2, w_index=layer, bias=b_pw2, residual=x, name="conv_pw2")


def _mlp(x, norm_g, w_in, w_out, layer):
    h = _rmsnorm(x, norm_g, BF16)
    a = _matmul(h, w_in, w_index=layer, act="relu2", out_dtype=BF16, name="mlp_in")
    return _matmul(a, w_out[layer].astype(BF16), residual=x, tm=1024, tn=1024, tk=2048, name="mlp_out")


def kernel(x, norm_mix_g, norm_mlp_g, norm_f_g, rwkv_mix, rwkv_w_rkv, rwkv_w0, rwkv_w1, rwkv_w2, rwkv_a0, rwkv_a1, rwkv_a2, rwkv_v0, rwkv_v1, rwkv_v2, rwkv_g1, rwkv_g2, rwkv_k_k, rwkv_k_a, rwkv_r_k, rwkv_lnx_g, rwkv_lnx_b, rwkv_w_o, swa_w_qkv, swa_b_qkv, swa_sinks, swa_w_o, swa_b_o, conv_w_pw1, conv_b_pw1, conv_w_dw, conv_b_dw, conv_ln_g, conv_ln_b, conv_w_pw2, conv_b_pw2, mlp_w_in, mlp_w_out):
    b, t, d = x.shape
    assert b == 1, "token shift and the WKV scan are written for a single sequence"
    depth = norm_mix_g.shape[0]
    xs = x.reshape(t, d)
    w_rkv = rwkv_w_rkv.reshape((-1,) + rwkv_w_rkv.shape[2:])
    ia = ib = ic = 0
    v_first = None
    for i in range(depth):
        kind = i % 3
        if kind == 0:
            vres = None if ia == 0 else (rwkv_v0[ia - 1], rwkv_v1[ia - 1], rwkv_v2[ia - 1])
            xs, v_first = _rwkv_layer(xs, norm_mix_g[i], v_first, vres, rwkv_mix[ia], w_rkv, ia,
                                      rwkv_w0[ia], rwkv_w1[ia], rwkv_w2[ia], rwkv_a0[ia], rwkv_a1[ia],
                                      rwkv_a2[ia], rwkv_g1[ia], rwkv_g2[ia], rwkv_k_k[ia], rwkv_k_a[ia],
                                      rwkv_r_k[ia], rwkv_lnx_g[ia], rwkv_lnx_b[ia], rwkv_w_o)
            ia += 1
        elif kind == 1:
            xs = _swa_layer(xs, norm_mix_g[i], swa_w_qkv[ib], swa_b_qkv[ib], swa_sinks[ib],
                            swa_w_o, ib, swa_b_o[ib])
            ib += 1
        else:
            xs = _conv_layer(xs, norm_mix_g[i], conv_w_pw1, ic, conv_b_pw1[ic], conv_w_dw[ic],
                             conv_b_dw[ic], conv_ln_g[ic], conv_ln_b[ic], conv_w_pw2, conv_b_pw2[ic])
            ic += 1
        xs = _mlp(xs, norm_mlp_g[i], mlp_w_in, mlp_w_out, i)
    return _rmsnorm(xs, norm_f_g, F32).reshape(b, t, d)
```

```python
import functools

import jax
import jax.numpy as jnp
from jax import lax
from jax.experimental import pallas as pl
from jax.experimental.pallas import tpu as pltpu

F32 = jnp.float32
BF16 = jnp.bfloat16

LANES = 128
HEAD = 64
CHUNK = 128
WKV_PAIRS = 8
WINDOW = 128
SWA_KV_HEADS = 8
CONV_WIDTH = 31
CONV_HALO = 32
RMS_EPS = 1e-5
LN_EPS = 1e-5
GN_EPS = 64e-5
NEG_INF = -1e30
VMEM_LIMIT = 56 * 1024 * 1024


def _cparams(sem):
    return pltpu.CompilerParams(dimension_semantics=sem, vmem_limit_bytes=VMEM_LIMIT)


def _tile(n, pref):
    if n <= pref:
        return n
    t = pref
    while n % t:
        t //= 2
    return t


def _rmsnorm_kernel(x_ref, g_ref, o_ref):
    x = x_ref[...]
    y = x * lax.rsqrt(jnp.mean(x * x, axis=-1, keepdims=True) + RMS_EPS)
    o_ref[...] = (y * g_ref[...]).astype(o_ref.dtype)


def _rmsnorm(x, g, out_dtype):
    m, d = x.shape
    tm = _tile(m, 256)
    return pl.pallas_call(
        _rmsnorm_kernel,
        grid=(m // tm,),
        in_specs=[pl.BlockSpec((tm, d), lambda i: (i, 0)),
                  pl.BlockSpec((1, d), lambda i: (0, 0))],
        out_specs=pl.BlockSpec((tm, d), lambda i: (i, 0)),
        out_shape=jax.ShapeDtypeStruct((m, d), out_dtype),
        compiler_params=_cparams(("arbitrary",)),
        name="rmsnorm",
    )(x, g.reshape(1, d))


def _rwkv_mix_kernel(x_ref, g_ref, mix_ref, wd_ref, o_ref, mid_ref, carry_ref, *, loras):
    @pl.when(pl.program_id(0) == 0)
    def _():
        carry_ref[...] = jnp.zeros_like(carry_ref)

    x = x_ref[...]
    h = x * lax.rsqrt(jnp.mean(x * x, axis=-1, keepdims=True) + RMS_EPS) * g_ref[...]
    tm = h.shape[0]
    prev = pltpu.roll(h, shift=1, axis=0)
    row = lax.broadcasted_iota(jnp.int32, h.shape, 0)
    prev = jnp.where(row == 0, carry_ref[...], prev)
    carry_ref[...] = h[tm - 1:tm, :]
    xx = prev - h
    streams = {}

    def stream(c):
        if c not in streams:
            streams[c] = (h + xx * mix_ref[c:c + 1, :]).astype(BF16)
        return streams[c]

    for c in range(3):
        o_ref[c] = stream(c)
    for c, off, width, act in loras:
        mid = jnp.dot(stream(c), wd_ref[:, off:off + width], preferred_element_type=F32)
        mid_ref[:, off:off + width] = _activate(mid, act).astype(mid_ref.dtype)


def _rwkv_mix(x, g, mix, w_down, loras):
    m, d = x.shape
    nl = w_down.shape[1]
    tm = _tile(m, 256)
    return pl.pallas_call(
        functools.partial(_rwkv_mix_kernel, loras=loras),
        grid=(m // tm,),
        in_specs=[pl.BlockSpec((tm, d), lambda i: (i, 0)),
                  pl.BlockSpec((1, d), lambda i: (0, 0)),
                  pl.BlockSpec((6, d), lambda i: (0, 0)),
                  pl.BlockSpec((d, nl), lambda i: (0, 0))],
        out_specs=[pl.BlockSpec((3, tm, d), lambda i: (0, i, 0)),
                   pl.BlockSpec((tm, nl), lambda i: (i, 0))],
        out_shape=[jax.ShapeDtypeStruct((3, m, d), BF16), jax.ShapeDtypeStruct((m, nl), BF16)],
        scratch_shapes=[pltpu.VMEM((1, d), F32)],
        compiler_params=_cparams(("arbitrary",)),
        name="rwkv_mix",
    )(x, g.reshape(1, d), mix, w_down)


def _activate(acc, act):
    if act is None:
        return acc
    if act == "tanh":
        return jnp.tanh(acc)
    if act == "sigmoid":
        return jax.nn.sigmoid(acc)
    if act == "relu2":
        return jnp.square(jnp.maximum(acc, 0.0))
    raise ValueError(act)


def _mm_kernel(*refs, nk, act, has_bias, has_res, cast_w, has_side):
    x_ref, w_ref = refs[0], refs[1]
    pos = 2
    b_ref = r_ref = None
    if has_bias:
        b_ref = refs[pos]
        pos += 1
    if has_res:
        r_ref = refs[pos]
        pos += 1
    if has_side:
        side_in_ref, side_out_ref = refs[pos], refs[pos + 2]
        side_out_ref[...] = side_in_ref[...].astype(side_out_ref.dtype)
        o_ref = refs[pos + 1]
        pos += 2
    else:
        o_ref = refs[pos]
    scratch = list(refs[pos + 1:])

    def epilogue(acc):
        if has_bias:
            acc = acc + b_ref[...]
        acc = _activate(acc, act)
        if has_res:
            acc = acc + r_ref[...]
        o_ref[...] = acc.astype(o_ref.dtype)

    if cast_w:
        wb_ref = scratch.pop(0)

        @pl.when(pl.program_id(1) == 0)
        def _():
            wb_ref[...] = w_ref[...].astype(BF16)

        w = wb_ref[...]
    else:
        w = w_ref[...]
    part = jnp.dot(x_ref[...], w, preferred_element_type=F32)
    if nk == 1:
        epilogue(part)
        return
    acc_ref = scratch.pop(0)
    k = pl.program_id(2)

    @pl.when(k == 0)
    def _():
        acc_ref[...] = part

    @pl.when(k > 0)
    def _():
        acc_ref[...] += part

    @pl.when(k == nk - 1)
    def _():
        epilogue(acc_ref[...])


def _matmul(x, w, *, name, x_index=None, x_col=0, kd=None, w_index=None, bias=None, act=None,
            residual=None, out_dtype=F32, tm=None, tn=None, tk=4096, side_cast=None):
    m = x.shape[-2]
    kd = x.shape[-1] if kd is None else kd
    n = w.shape[-1]
    assert w.shape[-2] == kd
    cast_w = w.dtype != BF16
    if tn is None:
        tn = 512 if cast_w else 1024
    if tm is None:
        tm = 1024 if cast_w else 512
    tm, tn, tk = _tile(m, tm), _tile(n, tn), _tile(kd, tk)
    nk = kd // tk
    assert not cast_w or nk == 1, "the cast-once path needs the weight block fixed across row tiles"
    assert x_col == 0 or nk == 1
    if w.ndim == 3:
        w_spec = pl.BlockSpec((None, tk, tn), lambda j, i, k: (w_index, k, j))
    else:
        w_spec = pl.BlockSpec((tk, tn), lambda j, i, k: (k, j))
    if x.ndim == 3:
        x_spec = pl.BlockSpec((None, tm, tk), lambda j, i, k: (x_index, i, k + x_col))
    else:
        x_spec = pl.BlockSpec((tm, tk), lambda j, i, k: (i, k + x_col))
    in_specs = [x_spec, w_spec]
    args = [x, w]
    if bias is not None:
        in_specs.append(pl.BlockSpec((1, tn), lambda j, i, k: (0, j)))
        args.append(bias.reshape(1, n).astype(F32))
    if residual is not None:
        in_specs.append(pl.BlockSpec((tm, tn), lambda j, i, k: (i, j)))
        args.append(residual)
    grid = (n // tn, m // tm, nk)
    out_specs = pl.BlockSpec((tm, tn), lambda j, i, k: (i, j))
    out_shape = jax.ShapeDtypeStruct((m, n), out_dtype)
    if side_cast is not None:
        stack, s_index = side_cast
        rows, cols = stack.shape[-2:]
        steps = grid[0] * grid[1]
        assert nk == 1 and rows % (steps * 16) == 0
        slab = rows // steps
        in_specs.append(pl.BlockSpec((None, slab, cols), lambda j, i, k: (s_index, j * grid[1] + i, 0)))
        args.append(stack)
        out_specs = [out_specs, pl.BlockSpec((slab, cols), lambda j, i, k: (j * grid[1] + i, 0))]
        out_shape = [out_shape, jax.ShapeDtypeStruct((rows, cols), BF16)]
    scratch = []
    if cast_w:
        scratch.append(pltpu.VMEM((tk, tn), BF16))
    if nk > 1:
        scratch.append(pltpu.VMEM((tm, tn), F32))
    return pl.pallas_call(
        functools.partial(_mm_kernel, nk=nk, act=act, has_bias=bias is not None,
                          has_res=residual is not None, cast_w=cast_w, has_side=side_cast is not None),
        grid=grid,
        in_specs=in_specs,
        out_specs=out_specs,
        out_shape=out_shape,
        scratch_shapes=scratch,
        compiler_params=_cparams(("arbitrary" if cast_w else "parallel",) * 2 + ("arbitrary",)),
        name=name,
    )(*args)


def _glu_mm_kernel(x_ref, wa_ref, wb_ref, ba_ref, bb_ref, o_ref, wab_ref, wbb_ref):
    @pl.when(pl.program_id(1) == 0)
    def _():
        wab_ref[...] = wa_ref[...].astype(BF16)
        wbb_ref[...] = wb_ref[...].astype(BF16)

    x = x_ref[...]
    a = jnp.dot(x, wab_ref[...], preferred_element_type=F32) + ba_ref[...]
    b = jnp.dot(x, wbb_ref[...], preferred_element_type=F32) + bb_ref[...]
    o_ref[...] = (a * jax.nn.sigmoid(b)).astype(o_ref.dtype)


def _glu_matmul(x, w, w_index, bias):
    m, kd = x.shape
    n = w.shape[-1] // 2
    tm, tn = _tile(m, 512), _tile(n, 256)
    nj = n // tn
    b2 = bias.reshape(1, 2 * n).astype(F32)
    return pl.pallas_call(
        _glu_mm_kernel,
        grid=(nj, m // tm),
        in_specs=[pl.BlockSpec((tm, kd), lambda j, i: (i, 0)),
                  pl.BlockSpec((None, kd, tn), lambda j, i: (w_index, 0, j)),
                  pl.BlockSpec((None, kd, tn), lambda j, i: (w_index, 0, j + nj)),
                  pl.BlockSpec((1, tn), lambda j, i: (0, j)),
                  pl.BlockSpec((1, tn), lambda j, i: (0, j + nj))],
        out_specs=pl.BlockSpec((tm, tn), lambda j, i: (i, j)),
        out_shape=jax.ShapeDtypeStruct((m, n), F32),
        scratch_shapes=[pltpu.VMEM((kd, tn), BF16), pltpu.VMEM((kd, tn), BF16)],
        compiler_params=_cparams(("arbitrary", "arbitrary")),
        name="conv_pw1_glu",
    )(x, w, w, b2, b2)


def _bdot(a, b):
    return jnp.dot(a.astype(BF16), b.astype(BF16), preferred_element_type=F32)


def _bdot_nt(a, b):
    return lax.dot_general(a.astype(BF16), b.astype(BF16), (((1,), (1,)), ((), ())),
                           preferred_element_type=F32)


def _split(a):
    hi = a.astype(BF16)
    lo = (a - hi.astype(F32)).astype(BF16)
    return hi, lo


def _blockdiag(y):
    left = lax.broadcasted_iota(jnp.int32, y.shape, 1) < y.shape[0]
    return jnp.concatenate([jnp.where(left, y, 0.0), jnp.where(left, 0.0, y)], axis=0)


def _hdot(x, y):
    return _bdot(x, _blockdiag(y))


def _unit_lower_inverses(n_list, eye2, xr2):
    n8 = [jnp.where(xr2 < 8, n, 0.0) for n in n_list]
    n8_2 = [_hdot(x, x) for x in n8]
    n8_4 = [_hdot(x, x) for x in n8_2]
    n8_3 = [_hdot(x, y) for x, y in zip(n8, n8_2)]
    t = [eye2 + a + b + c for a, b, c in zip(n8, n8_2, n8_3)]
    t = [x + _hdot(x, y) for x, y in zip(t, n8_4)]
    k = 8
    while k < CHUNK:
        level = (xr2 >= k) & (xr2 < 2 * k)
        u = [_hdot(jnp.where(level, n, 0.0), x) for n, x in zip(n_list, t)]
        t = [x + _hdot(x, y) for x, y in zip(t, u)]
        k *= 2
    return t


def _wkv_kernel(*refs, pairs, has_vres):
    if has_vres:
        (r_ref, k_ref, v_ref, wl_ref, al_ref, g_ref, vf_ref, vl_ref,
         w0_ref, a0_ref, kk_ref, ka_ref, rk_ref, lg_ref, lb_ref, v0_ref, o_ref, s_ref) = refs
    else:
        (r_ref, k_ref, v_ref, wl_ref, al_ref, g_ref,
         w0_ref, a0_ref, kk_ref, ka_ref, rk_ref, lg_ref, lb_ref, o_ref, s_ref) = refs
        vf_ref = vl_ref = v0_ref = None

    @pl.when(pl.program_id(1) == 0)
    def _():
        s_ref[...] = jnp.zeros_like(s_ref)

    L = CHUNK
    row = lax.broadcasted_iota(jnp.int32, (L, L), 0)
    col = lax.broadcasted_iota(jnp.int32, (L, L), 1)
    lower = row >= col
    same_head = (row >= HEAD) == (col >= HEAD)
    eye = (row == col).astype(F32)
    ltri = lower.astype(BF16)
    seg = same_head.astype(BF16)
    row2 = lax.broadcasted_iota(jnp.int32, (L, 2 * L), 0)
    col2 = lax.broadcasted_iota(jnp.int32, (L, 2 * L), 1) % L
    lower2 = row2 >= col2
    strict2 = row2 > col2
    xr2 = row2 ^ col2
    eye2 = (row2 == col2).astype(F32)

    def both(a):
        h0 = lax.broadcasted_iota(jnp.int32, a.shape, 1) % LANES < HEAD
        return jnp.concatenate([jnp.where(h0, a, 0.0), jnp.where(h0, 0.0, a)], axis=0)

    slabs = [slice(p * LANES, (p + 1) * LANES) for p in range(pairs)]

    def segsum(x):
        return jnp.concatenate([_bdot(x[:, sl], seg) for sl in slabs], axis=1)

    r = r_ref[...]
    k = k_ref[...]
    v = v_ref[...]
    z = w0_ref[...] + wl_ref[...]
    w = jnp.minimum(z, 0.0) - jnp.log(1.0 + jnp.exp(-jnp.abs(z))) - 0.5
    lw = -jnp.exp(w)
    a = jax.nn.sigmoid(a0_ref[...] + al_ref[...])
    if has_vres:
        v = v + (vf_ref[...] - v) * jax.nn.sigmoid(v0_ref[...] + vl_ref[...])
    kk = k * kk_ref[...]
    kk = kk / jnp.maximum(jnp.sqrt(segsum(kk * kk)), 1e-12)
    k = k * (1.0 + (a - 1.0) * ka_ref[...])
    avec = -kk
    bvec = kk * a

    lw_hi, lw_lo = _split(lw)
    cum = (jnp.dot(ltri, lw_hi, preferred_element_type=F32)
           + jnp.dot(ltri, lw_lo, preferred_element_type=F32))
    cmid = cum[L // 2 - 1:L // 2, :]
    clast = cum[L - 1:L, :]
    e_pos = jnp.exp(cum - cmid)
    e_neg = jnp.exp(cmid - cum)
    e_prev = jnp.exp(cum - lw - cmid)
    e_last = jnp.exp(clast - cum)
    p_last = jnp.exp(clast)
    p_mid = jnp.exp(cmid)
    r_h = r * e_pos
    a_h = avec * e_prev
    b_t = bvec * e_neg
    k_t = k * e_neg
    b_d = bvec * e_last
    k_d = k * e_last

    def sbs(att, r0, c0):
        return jnp.concatenate([att[r0:r0 + L, c0:c0 + L], att[r0 + 2 * L:r0 + 3 * L, c0:c0 + L]], axis=1)

    att = [_bdot_nt(both(jnp.concatenate([a_h[:, sl], r_h[:, sl]], axis=0)),
                    jnp.concatenate([b_t[:, sl], k_t[:, sl]], axis=0))
           for sl in slabs]
    a_ab = [jnp.where(strict2, sbs(x, 0, 0), 0.0) for x in att]
    a_ak = [jnp.where(strict2, sbs(x, 0, L), 0.0) for x in att]
    a_rb = [jnp.where(lower2, sbs(x, L, 0), 0.0) for x in att]
    a_rk = [jnp.where(lower2, sbs(x, L, L), 0.0) for x in att]
    t_inv = _unit_lower_inverses(a_ab, eye2, xr2)

    v2 = [both(v[:, sl]) for sl in slabs]
    akv = [_bdot(x, y) for x, y in zip(a_ak, v2)]
    wu = [_bdot(t, both(jnp.concatenate([a_h[:, sl], u], axis=1)))
          for t, u, sl in zip(t_inv, akv, slabs)]
    qy = [_bdot(x, both(y)) for x, y in zip(a_rb, wu)]
    yk = [_bdot(x, y) for x, y in zip(a_rk, v2)]

    y_parts = []
    for p, sl in enumerate(slabs):
        q_t = (r_h[:, sl] + qy[p][:, :LANES]) * p_mid[:, sl]
        y_in = qy[p][:, LANES:] + yk[p]
        bdt = b_d[:, sl].T
        kdt = k_d[:, sl].T
        m2 = eye * p_last[:, sl] + jnp.where(
            same_head, _bdot(bdt, wu[p][:, :LANES] * p_mid[:, sl]), 0.0)
        g2 = jnp.where(same_head,
                       _bdot(jnp.concatenate([bdt, kdt], axis=1),
                             jnp.concatenate([wu[p][:, LANES:], v[:, sl]], axis=0)), 0.0)
        s = s_ref[p]
        y_parts.append(_bdot(q_t, s) + y_in)
        s_ref[p] = _bdot(m2, s) + g2
    y = jnp.concatenate(y_parts, axis=1)

    mu = segsum(y) * (1.0 / HEAD)
    d = y - mu
    var = segsum(d * d) * (1.0 / HEAD)
    yn = d * lax.rsqrt(var + GN_EPS) * lg_ref[...] + lb_ref[...]
    bonus = segsum(r * k * rk_ref[...]) * v
    o_ref[...] = ((yn + bonus) * g_ref[...]).astype(o_ref.dtype)


def _wkv7(r, k, v, wl, al, g, w0, a0, k_k, k_a, r_k, lnx_g, lnx_b, vres=None):
    t, d = r.shape
    pairs = min(WKV_PAIRS, d // LANES)
    pw = pairs * LANES
    assert t % CHUNK == 0 and d % pw == 0
    seq = pl.BlockSpec((CHUNK, pw), lambda j, c: (c, j))
    par = pl.BlockSpec((1, pw), lambda j, c: (0, j))
    row = lambda a: a.reshape(1, d).astype(F32)
    seq_args = [r, k, v, wl, al, g]
    par_args = [row(w0), row(a0), row(k_k), row(k_a), row(r_k), row(lnx_g), row(lnx_b)]
    if vres is not None:
        v_first, vl, v0 = vres
        seq_args += [v_first, vl]
        par_args += [row(v0)]
    return pl.pallas_call(
        functools.partial(_wkv_kernel, pairs=pairs, has_vres=vres is not None),
        grid=(d // pw, t // CHUNK),
        in_specs=[seq] * len(seq_args) + [par] * len(par_args),
        out_specs=seq,
        out_shape=jax.ShapeDtypeStruct((t, d), BF16),
        scratch_shapes=[pltpu.VMEM((pairs, LANES, LANES), F32)],
        compiler_params=_cparams(("parallel", "arbitrary")),
        name="wkv7",
    )(*seq_args, *par_args)


def _swa_kernel(sink_ref, q_ref, kc_ref, kp_ref, vc_ref, vp_ref, o_ref, *, group):
    n = pl.program_id(0)
    W = WINDOW
    qi = lax.broadcasted_iota(jnp.int32, (2 * W, 2 * W), 0) % W
    kj = lax.broadcasted_iota(jnp.int32, (2 * W, 2 * W), 1)
    valid = (kj > qi) & (kj <= qi + W) & ((kj >= W) | (n > 0))
    top = lax.broadcasted_iota(jnp.int32, (2 * W, 1), 0) < W
    lane = lax.broadcasted_iota(jnp.int32, (W, LANES), 1)
    head0 = lane < HEAD
    zero = jnp.zeros((), BF16)
    for hk in range(SWA_KV_HEADS):
        ks = slice(hk * LANES, (hk + 1) * LANES)
        k2 = jnp.concatenate([kp_ref[:, ks], kc_ref[:, ks]], axis=0)
        v2 = jnp.concatenate([vp_ref[:, ks], vc_ref[:, ks]], axis=0)
        for i in range(group // 2):
            slab = hk * (group // 2) + i
            qs = q_ref[:, slab * LANES:(slab + 1) * LANES]
            q2 = jnp.concatenate([jnp.where(head0, qs, zero), jnp.where(head0, zero, qs)], axis=0)
            s = lax.dot_general(q2, k2, (((1,), (1,)), ((), ())), preferred_element_type=F32)
            s = jnp.where(valid, s, NEG_INF)
            sink = jnp.where(top, sink_ref[2 * slab], sink_ref[2 * slab + 1])
            m = jnp.maximum(jnp.max(s, axis=-1, keepdims=True), sink)
            e = jnp.exp(s - m)
            den = jnp.sum(e, axis=-1, keepdims=True) + jnp.exp(sink - m)
            pr = (e / den).astype(BF16)
            o = jnp.dot(pr, v2, preferred_element_type=F32)
            o_ref[:, slab * LANES:(slab + 1) * LANES] = jnp.where(head0, o[:W], o[W:]).astype(o_ref.dtype)


def _swa_attention(qkv, sinks, d):
    t = qkv.shape[0]
    kvw = SWA_KV_HEADS * LANES
    qb = d // kvw
    group = d // HEAD // SWA_KV_HEADS
    assert group % 2 == 0 and d % kvw == 0 and t % WINDOW == 0
    prev = lambda n: jnp.maximum(n - 1, 0)
    return pl.pallas_call(
        functools.partial(_swa_kernel, group=group),
        grid=(t // WINDOW,),
        in_specs=[pl.BlockSpec(memory_space=pltpu.SMEM),
                  pl.BlockSpec((WINDOW, d), lambda n: (n, 0)),
                  pl.BlockSpec((WINDOW, kvw), lambda n: (n, qb)),
                  pl.BlockSpec((WINDOW, kvw), lambda n: (prev(n), qb)),
                  pl.BlockSpec((WINDOW, kvw), lambda n: (n, qb + 1)),
                  pl.BlockSpec((WINDOW, kvw), lambda n: (prev(n), qb + 1))],
        out_specs=pl.BlockSpec((WINDOW, d), lambda n: (n, 0)),
        out_shape=jax.ShapeDtypeStruct((t, d), BF16),
        compiler_params=_cparams(("arbitrary",)),
        name="swa_attention",
    )(sinks.astype(F32), qkv, qkv, qkv, qkv, qkv)


def _conv_kernel(u_ref, halo_ref, w_ref, b_ref, g_ref, beta_ref, o_ref, buf_ref, acc_ref):
    tm, d = u_ref.shape
    first = pl.program_id(0) == 0
    buf_ref[0:CONV_HALO, :] = jnp.where(first, 0.0, halo_ref[...])
    buf_ref[CONV_HALO:, :] = u_ref[...]
    rb = min(tm, 64)
    base = CONV_HALO - (CONV_WIDTH - 1)

    def col_body(c, carry):
        cs = pl.ds(pl.multiple_of(c * LANES, LANES), LANES)
        for r0 in range(0, tm, rb):
            acc = jnp.zeros((rb, LANES), F32)
            for j in range(CONV_WIDTH):
                acc = acc + buf_ref[pl.ds(r0 + base + j, rb), cs] * w_ref[pl.ds(j, 1), cs]
            acc_ref[pl.ds(r0, rb), cs] = acc
        return carry

    lax.fori_loop(0, d // LANES, col_body, 0)

    slab = min(tm, 16)

    def row_body(s, carry):
        rs = pl.ds(pl.multiple_of(s * slab, slab), slab)
        y = acc_ref[rs, :] + b_ref[...]
        mu = jnp.mean(y, axis=-1, keepdims=True)
        yc = y - mu
        var = jnp.mean(yc * yc, axis=-1, keepdims=True)
        yn = yc * lax.rsqrt(var + LN_EPS) * g_ref[...] + beta_ref[...]
        o_ref[rs, :] = (yn * jax.nn.sigmoid(yn)).astype(o_ref.dtype)
        return carry

    lax.fori_loop(0, tm // slab, row_body, 0)


def _conv_ln_silu(u, w_dw, b_dw, ln_g, ln_b):
    t, d = u.shape
    tm = _tile(t, 256)
    assert tm % CONV_HALO == 0
    ratio = tm // CONV_HALO
    row = lambda a: a.reshape(1, d).astype(F32)
    vec = pl.BlockSpec((1, d), lambda i: (0, 0))
    return pl.pallas_call(
        _conv_kernel,
        grid=(t // tm,),
        in_specs=[pl.BlockSpec((tm, d), lambda i: (i, 0)),
                  pl.BlockSpec((CONV_HALO, d), lambda i: (jnp.maximum(i * ratio - 1, 0), 0)),
                  pl.BlockSpec((CONV_WIDTH, d), lambda i: (0, 0)),
                  vec, vec, vec],
        out_specs=pl.BlockSpec((tm, d), lambda i: (i, 0)),
        out_shape=jax.ShapeDtypeStruct((t, d), BF16),
        scratch_shapes=[pltpu.VMEM((tm + CONV_HALO, d), F32), pltpu.VMEM((tm, d), F32)],
        compiler_params=_cparams(("arbitrary",)),
        name="conv_dw_ln_silu",
    )(u, u, w_dw.astype(F32), row(b_dw), row(ln_g), row(ln_b))


def _pad_to(a, axis, mult):
    pad = (-a.shape[axis]) % mult
    if pad == 0:
        return a
    widths = [(0, 0)] * a.ndim
    widths[axis] = (0, pad)
    return jnp.pad(a, widths)


def _rwkv_layer(x, norm_g, v_first, vres, mix, w_rkv, layer, w0, w1, w2, a0, a1, a2, g1, g2,
                k_k, k_a, r_k, lnx_g, lnx_b, w_o):
    d = x.shape[1]
    branches = [(5, g1, g2, "sigmoid"), (3, w1, w2, "tanh"), (4, a1, a2, None)]
    if vres is not None:
        branches.append((2, vres[1], vres[2], None))
    downs = [_pad_to(b[1], 1, LANES) for b in branches]
    widths = [w.shape[1] for w in downs]
    offs = [sum(widths[:i]) for i in range(len(widths))]
    assert all(o % w == 0 for o, w in zip(offs, widths))
    w_down = jnp.concatenate(downs, axis=1).astype(BF16)
    loras = tuple((b[0], o, w, b[3]) for b, o, w in zip(branches, offs, widths))
    xs, mids = _rwkv_mix(x, norm_g, mix, w_down, loras)
    r, k, v = (_matmul(xs, w_rkv, x_index=c, w_index=3 * layer + c, name="rwkv_" + "rkv"[c])
               for c in range(3))
    ups = [_matmul(mids, _pad_to(b[2], 0, LANES).astype(BF16), x_col=o // w, kd=w, name="rwkv_lora_up")
           for b, o, w in zip(branches, offs, widths)]
    g, wl, al = ups[:3]
    if vres is None:
        v_first = v
        vres_args = None
    else:
        vres_args = (v_first, ups[3], vres[0])
    z = _wkv7(r, k, v, wl, al, g, w0, a0, k_k, k_a, r_k.reshape(d), lnx_g, lnx_b, vres=vres_args)
    return _matmul(z, w_o, w_index=layer, residual=x, name="rwkv_out"), v_first


def _swa_layer(x, norm_g, w_qkv, b_qkv, sinks, w_o, layer, b_o):
    d = x.shape[1]
    kvw = SWA_KV_HEADS * HEAD
    scale = HEAD ** -0.5

    def dup(a):
        a = a.reshape(a.shape[:-1] + (SWA_KV_HEADS, 1, HEAD))
        a = jnp.broadcast_to(a, a.shape[:-2] + (2, HEAD))
        return a.reshape(a.shape[:-3] + (2 * kvw,))

    w_ext = jnp.concatenate([w_qkv[:, :d] * scale, dup(w_qkv[:, d:d + kvw]), dup(w_qkv[:, d + kvw:])], axis=1)
    b_ext = jnp.concatenate([b_qkv[:d] * scale, dup(b_qkv[d:d + kvw]), dup(b_qkv[d + kvw:])], axis=0)
    h = _rmsnorm(x, norm_g, BF16)
    qkv = _matmul(h, w_ext.astype(BF16), bias=b_ext, out_dtype=BF16, name="swa_qkv")
    o = _swa_attention(qkv, sinks, d)
    return _matmul(o, w_o, w_index=layer, bias=b_o, residual=x, name="swa_out")


def _conv_layer(x, norm_g, w_pw1, layer, b_pw1, w_dw, b_dw, ln_g, ln_b, w_pw2, b_pw2):
    h = _rmsnorm(x, norm_g, BF16)
    u = _glu_matmul(h, w_pw1, layer, b_pw1)
    u = _conv_ln_silu(u, w_dw, b_dw, ln_g, ln_b)
    return _matmul(u, w_pw2, w_index=layer, bias=b_pw2, residual=x, name="conv_pw2")


def _mlp(x, norm_g, w_in, w_out, layer):
    h = _rmsnorm(x, norm_g, BF16)
    a, w_out_bf16 = _matmul(h, w_in, w_index=layer, act="relu2", out_dtype=BF16,
                            side_cast=(w_out, layer), name="mlp_in")
    return _matmul(a, w_out_bf16, residual=x, tm=1024, tn=1024, tk=2048, name="mlp_out")


def kernel(x, norm_mix_g, norm_mlp_g, norm_f_g, rwkv_mix, rwkv_w_rkv, rwkv_w0, rwkv_w1, rwkv_w2, rwkv_a0, rwkv_a1, rwkv_a2, rwkv_v0, rwkv_v1, rwkv_v2, rwkv_g1, rwkv_g2, rwkv_k_k, rwkv_k_a, rwkv_r_k, rwkv_lnx_g, rwkv_lnx_b, rwkv_w_o, swa_w_qkv, swa_b_qkv, swa_sinks, swa_w_o, swa_b_o, conv_w_pw1, conv_b_pw1, conv_w_dw, conv_b_dw, conv_ln_g, conv_ln_b, conv_w_pw2, conv_b_pw2, mlp_w_in, mlp_w_out):
    b, t, d = x.shape
    assert b == 1, "token shift and the WKV scan are written for a single sequence"
    depth = norm_mix_g.shape[0]
    xs = x.reshape(t, d)
    w_rkv = rwkv_w_rkv.reshape((-1,) + rwkv_w_rkv.shape[2:])
    ia = ib = ic = 0
    v_first = None
    for i in range(depth):
        kind = i % 3
        if kind == 0:
            vres = None if ia == 0 else (rwkv_v0[ia - 1], rwkv_v1[ia - 1], rwkv_v2[ia - 1])
            xs, v_first = _rwkv_layer(xs, norm_mix_g[i], v_first, vres, rwkv_mix[ia], w_rkv, ia,
                                      rwkv_w0[ia], rwkv_w1[ia], rwkv_w2[ia], rwkv_a0[ia], rwkv_a1[ia],
                                      rwkv_a2[ia], rwkv_g1[ia], rwkv_g2[ia], rwkv_k_k[ia], rwkv_k_a[ia],
                                      rwkv_r_k[ia], rwkv_lnx_g[ia], rwkv_lnx_b[ia], rwkv_w_o)
            ia += 1
        elif kind == 1:
            xs = _swa_layer(xs, norm_mix_g[i], swa_w_qkv[ib], swa_b_qkv[ib], swa_sinks[ib],
                            swa_w_o, ib, swa_b_o[ib])
            ib += 1
        else:
            xs = _conv_layer(xs, norm_mix_g[i], conv_w_pw1, ic, conv_b_pw1[ic], conv_w_dw[ic],
                             conv_b_dw[ic], conv_ln_g[ic], conv_ln_b[ic], conv_w_pw2, conv_b_pw2[ic])
            ic += 1
        xs = _mlp(xs, norm_mlp_g[i], mlp_w_in, mlp_w_out, i)
    return _rmsnorm(xs, norm_f_g, F32).reshape(b, t, d)
```

```python
import functools

import jax
import jax.numpy as jnp
from jax import lax
from jax.experimental import pallas as pl
from jax.experimental.pallas import tpu as pltpu

F32 = jnp.float32
BF16 = jnp.bfloat16

LANES = 128
SUBLANES = 8
HEAD = 64
CHUNK = 128
WKV_PAIRS = 16
WINDOW = 128
SWA_KV_HEADS = 8
CONV_WIDTH = 31
CONV_HALO = 32
RMS_EPS = 1e-5
LN_EPS = 1e-5
GN_EPS = 64e-5
NEG_INF = -1e30
VMEM_LIMIT = 56 * 1024 * 1024


def _cparams(sem):
    return pltpu.CompilerParams(dimension_semantics=sem, vmem_limit_bytes=VMEM_LIMIT)


def _tile(n, pref):
    if n <= pref:
        return n
    t = pref
    while n % t:
        t //= 2
    return t


def _rmsnorm_kernel(x_ref, g_ref, o_ref):
    x = x_ref[...]
    y = x * lax.rsqrt(jnp.mean(x * x, axis=-1, keepdims=True) + RMS_EPS)
    o_ref[...] = (y * g_ref[...]).astype(o_ref.dtype)


def _rmsnorm(x, g, out_dtype):
    m, d = x.shape
    tm = _tile(m, 256)
    return pl.pallas_call(
        _rmsnorm_kernel,
        grid=(m // tm,),
        in_specs=[pl.BlockSpec((tm, d), lambda i: (i, 0)),
                  pl.BlockSpec((1, d), lambda i: (0, 0))],
        out_specs=pl.BlockSpec((tm, d), lambda i: (i, 0)),
        out_shape=jax.ShapeDtypeStruct((m, d), out_dtype),
        compiler_params=_cparams(("arbitrary",)),
        name="rmsnorm",
    )(x, g.reshape(1, d))


def _rwkv_mix_kernel(x_ref, g_ref, mix_ref, wd_ref, o_ref, mid_ref, carry_ref, *, loras):
    @pl.when(pl.program_id(0) == 0)
    def _():
        carry_ref[...] = jnp.zeros_like(carry_ref)

    x = x_ref[...]
    h = x * lax.rsqrt(jnp.mean(x * x, axis=-1, keepdims=True) + RMS_EPS) * g_ref[...]
    tm = h.shape[0]
    prev = pltpu.roll(h, shift=1, axis=0)
    row = lax.broadcasted_iota(jnp.int32, h.shape, 0)
    prev = jnp.where(row == 0, carry_ref[...], prev)
    carry_ref[...] = h[tm - 1:tm, :]
    xx = prev - h
    streams = {}

    def stream(c):
        if c not in streams:
            streams[c] = (h + xx * mix_ref[c:c + 1, :]).astype(BF16)
        return streams[c]

    for c in range(3):
        o_ref[c] = stream(c)
    for c, off, width, act in loras:
        mid = jnp.dot(stream(c), wd_ref[:, off:off + width], preferred_element_type=F32)
        mid_ref[:, off:off + width] = _activate(mid, act).astype(mid_ref.dtype)


def _rwkv_mix(x, g, mix, w_down, loras):
    m, d = x.shape
    nl = w_down.shape[1]
    tm = _tile(m, 256)
    return pl.pallas_call(
        functools.partial(_rwkv_mix_kernel, loras=loras),
        grid=(m // tm,),
        in_specs=[pl.BlockSpec((tm, d), lambda i: (i, 0)),
                  pl.BlockSpec((1, d), lambda i: (0, 0)),
                  pl.BlockSpec((6, d), lambda i: (0, 0)),
                  pl.BlockSpec((d, nl), lambda i: (0, 0))],
        out_specs=[pl.BlockSpec((3, tm, d), lambda i: (0, i, 0)),
                   pl.BlockSpec((tm, nl), lambda i: (i, 0))],
        out_shape=[jax.ShapeDtypeStruct((3, m, d), BF16), jax.ShapeDtypeStruct((m, nl), BF16)],
        scratch_shapes=[pltpu.VMEM((1, d), F32)],
        compiler_params=_cparams(("arbitrary",)),
        name="rwkv_mix",
    )(x, g.reshape(1, d), mix, w_down)


def _activate(acc, act):
    if act is None:
        return acc
    if act == "tanh":
        return jnp.tanh(acc)
    if act == "sigmoid":
        return jax.nn.sigmoid(acc)
    if act == "relu2":
        return jnp.square(jnp.maximum(acc, 0.0))
    raise ValueError(act)


def _mm_kernel(*refs, nk, act, has_bias, has_res, cast_w, has_side):
    x_ref, w_ref = refs[0], refs[1]
    pos = 2
    b_ref = r_ref = None
    if has_bias:
        b_ref = refs[pos]
        pos += 1
    if has_res:
        r_ref = refs[pos]
        pos += 1
    if has_side:
        side_in_ref, side_out_ref = refs[pos], refs[pos + 2]
        side_out_ref[...] = side_in_ref[...].astype(side_out_ref.dtype)
        o_ref = refs[pos + 1]
        pos += 2
    else:
        o_ref = refs[pos]
    scratch = list(refs[pos + 1:])

    def epilogue(acc):
        if has_bias:
            acc = acc + b_ref[...]
        acc = _activate(acc, act)
        if has_res:
            acc = acc + r_ref[...]
        o_ref[...] = acc.astype(o_ref.dtype)

    if cast_w:
        wb_ref = scratch.pop(0)

        @pl.when(pl.program_id(1) == 0)
        def _():
            wb_ref[...] = w_ref[...].astype(BF16)

        w = wb_ref[...]
    else:
        w = w_ref[...]
    part = jnp.dot(x_ref[...], w, preferred_element_type=F32)
    if nk == 1:
        epilogue(part)
        return
    acc_ref = scratch.pop(0)
    k = pl.program_id(2)

    @pl.when(k == 0)
    def _():
        acc_ref[...] = part

    @pl.when(k > 0)
    def _():
        acc_ref[...] += part

    @pl.when(k == nk - 1)
    def _():
        epilogue(acc_ref[...])


def _matmul(x, w, *, name, x_index=None, x_col=0, kd=None, w_index=None, bias=None, act=None,
            residual=None, out_dtype=F32, tm=None, tn=None, tk=4096, side_cast=None):
    m = x.shape[-2]
    kd = x.shape[-1] if kd is None else kd
    n = w.shape[-1]
    assert w.shape[-2] == kd
    cast_w = w.dtype != BF16
    if tn is None:
        tn = 512 if cast_w else 1024
    if tm is None:
        tm = 1024 if cast_w else 512
    tm, tn, tk = _tile(m, tm), _tile(n, tn), _tile(kd, tk)
    nk = kd // tk
    assert not cast_w or nk == 1, "the cast-once path needs the weight block fixed across row tiles"
    assert x_col == 0 or nk == 1
    if w.ndim == 3:
        w_spec = pl.BlockSpec((None, tk, tn), lambda j, i, k: (w_index, k, j))
    else:
        w_spec = pl.BlockSpec((tk, tn), lambda j, i, k: (k, j))
    if x.ndim == 3:
        x_spec = pl.BlockSpec((None, tm, tk), lambda j, i, k: (x_index, i, k + x_col))
    else:
        x_spec = pl.BlockSpec((tm, tk), lambda j, i, k: (i, k + x_col))
    in_specs = [x_spec, w_spec]
    args = [x, w]
    if bias is not None:
        in_specs.append(pl.BlockSpec((1, tn), lambda j, i, k: (0, j)))
        args.append(bias.reshape(1, n).astype(F32))
    if residual is not None:
        in_specs.append(pl.BlockSpec((tm, tn), lambda j, i, k: (i, j)))
        args.append(residual)
    grid = (n // tn, m // tm, nk)
    out_specs = pl.BlockSpec((tm, tn), lambda j, i, k: (i, j))
    out_shape = jax.ShapeDtypeStruct((m, n), out_dtype)
    if side_cast is not None:
        stack, s_index = side_cast
        rows, cols = stack.shape[-2:]
        steps = grid[0] * grid[1]
        assert nk == 1 and rows % (steps * 16) == 0
        slab = rows // steps
        in_specs.append(pl.BlockSpec((None, slab, cols), lambda j, i, k: (s_index, j * grid[1] + i, 0)))
        args.append(stack)
        out_specs = [out_specs, pl.BlockSpec((slab, cols), lambda j, i, k: (j * grid[1] + i, 0))]
        out_shape = [out_shape, jax.ShapeDtypeStruct((rows, cols), BF16)]
    scratch = []
    if cast_w:
        scratch.append(pltpu.VMEM((tk, tn), BF16))
    if nk > 1:
        scratch.append(pltpu.VMEM((tm, tn), F32))
    return pl.pallas_call(
        functools.partial(_mm_kernel, nk=nk, act=act, has_bias=bias is not None,
                          has_res=residual is not None, cast_w=cast_w, has_side=side_cast is not None),
        grid=grid,
        in_specs=in_specs,
        out_specs=out_specs,
        out_shape=out_shape,
        scratch_shapes=scratch,
        compiler_params=_cparams(("arbitrary" if cast_w else "parallel",) * 2 + ("arbitrary",)),
        name=name,
    )(*args)


def _glu_mm_kernel(x_ref, wa_ref, wb_ref, ba_ref, bb_ref, o_ref, wab_ref, wbb_ref):
    @pl.when(pl.program_id(1) == 0)
    def _():
        wab_ref[...] = wa_ref[...].astype(BF16)
        wbb_ref[...] = wb_ref[...].astype(BF16)

    x = x_ref[...]
    a = jnp.dot(x, wab_ref[...], preferred_element_type=F32) + ba_ref[...]
    b = jnp.dot(x, wbb_ref[...], preferred_element_type=F32) + bb_ref[...]
    o_ref[...] = (a * jax.nn.sigmoid(b)).astype(o_ref.dtype)


def _glu_matmul(x, w, w_index, bias):
    m, kd = x.shape
    n = w.shape[-1] // 2
    tm, tn = _tile(m, 512), _tile(n, 256)
    nj = n // tn
    b2 = bias.reshape(1, 2 * n).astype(F32)
    return pl.pallas_call(
        _glu_mm_kernel,
        grid=(nj, m // tm),
        in_specs=[pl.BlockSpec((tm, kd), lambda j, i: (i, 0)),
                  pl.BlockSpec((None, kd, tn), lambda j, i: (w_index, 0, j)),
                  pl.BlockSpec((None, kd, tn), lambda j, i: (w_index, 0, j + nj)),
                  pl.BlockSpec((1, tn), lambda j, i: (0, j)),
                  pl.BlockSpec((1, tn), lambda j, i: (0, j + nj))],
        out_specs=pl.BlockSpec((tm, tn), lambda j, i: (i, j)),
        out_shape=jax.ShapeDtypeStruct((m, n), F32),
        scratch_shapes=[pltpu.VMEM((kd, tn), BF16), pltpu.VMEM((kd, tn), BF16)],
        compiler_params=_cparams(("arbitrary", "arbitrary")),
        name="conv_pw1_glu",
    )(x, w, w, b2, b2)


def _bdot(a, b):
    return jnp.dot(a.astype(BF16), b.astype(BF16), preferred_element_type=F32)


def _bdot_nt(a, b):
    return lax.dot_general(a.astype(BF16), b.astype(BF16), (((1,), (1,)), ((), ())),
                           preferred_element_type=F32)


def _split(a):
    hi = a.astype(BF16)
    lo = (a - hi.astype(F32)).astype(BF16)
    return hi, lo


def _blockdiag(y):
    left = lax.broadcasted_iota(jnp.int32, y.shape, 1) < y.shape[0]
    return jnp.concatenate([jnp.where(left, y, 0.0), jnp.where(left, 0.0, y)], axis=0)


def _hdot(x, y):
    return _bdot(x, _blockdiag(y))


def _unit_lower_inverses(n_list, eye2, xr2):
    n8 = [jnp.where(xr2 < 8, n, 0.0) for n in n_list]
    n8_2 = [_hdot(x, x) for x in n8]
    n8_4 = [_hdot(x, x) for x in n8_2]
    n8_3 = [_hdot(x, y) for x, y in zip(n8, n8_2)]
    t = [eye2 + a + b + c for a, b, c in zip(n8, n8_2, n8_3)]
    t = [x + _hdot(x, y) for x, y in zip(t, n8_4)]
    k = 8
    while k < CHUNK:
        level = (xr2 >= k) & (xr2 < 2 * k)
        u = [_hdot(jnp.where(level, n, 0.0), x) for n, x in zip(n_list, t)]
        t = [x + _hdot(x, y) for x, y in zip(t, u)]
        k *= 2
    return t


def _wkv_kernel(*refs, pairs, has_vres):
    if has_vres:
        (r_ref, k_ref, v_ref, wl_ref, al_ref, g_ref, vf_ref, vl_ref,
         w0_ref, a0_ref, kk_ref, ka_ref, rk_ref, lg_ref, lb_ref, v0_ref, o_ref, s_ref) = refs
    else:
        (r_ref, k_ref, v_ref, wl_ref, al_ref, g_ref,
         w0_ref, a0_ref, kk_ref, ka_ref, rk_ref, lg_ref, lb_ref, o_ref, s_ref) = refs
        vf_ref = vl_ref = v0_ref = None

    @pl.when(pl.program_id(1) == 0)
    def _():
        s_ref[...] = jnp.zeros_like(s_ref)

    L = CHUNK
    row = lax.broadcasted_iota(jnp.int32, (L, L), 0)
    col = lax.broadcasted_iota(jnp.int32, (L, L), 1)
    lower = row >= col
    same_head = (row >= HEAD) == (col >= HEAD)
    eye = (row == col).astype(F32)
    ltri = lower.astype(BF16)
    seg = same_head.astype(BF16)
    row2 = lax.broadcasted_iota(jnp.int32, (L, 2 * L), 0)
    col2 = lax.broadcasted_iota(jnp.int32, (L, 2 * L), 1) % L
    lower2 = row2 >= col2
    strict2 = row2 > col2
    xr2 = row2 ^ col2
    eye2 = (row2 == col2).astype(F32)

    def both(a):
        h0 = lax.broadcasted_iota(jnp.int32, a.shape, 1) % LANES < HEAD
        return jnp.concatenate([jnp.where(h0, a, 0.0), jnp.where(h0, 0.0, a)], axis=0)

    slabs = [slice(p * LANES, (p + 1) * LANES) for p in range(pairs)]

    def segsum(x):
        return _bdot(x, seg)

    def prepare(sl):
        r = r_ref[:, sl]
        k = k_ref[:, sl]
        v = v_ref[:, sl]
        z = w0_ref[:, sl] + wl_ref[:, sl]
        w = jnp.minimum(z, 0.0) - jnp.log(1.0 + jnp.exp(-jnp.abs(z))) - 0.5
        lw = -jnp.exp(w)
        a = jax.nn.sigmoid(a0_ref[:, sl] + al_ref[:, sl])
        if has_vres:
            v = v + (vf_ref[:, sl] - v) * jax.nn.sigmoid(v0_ref[:, sl] + vl_ref[:, sl])
        kk = k * kk_ref[:, sl]
        kk = kk / jnp.maximum(jnp.sqrt(segsum(kk * kk)), 1e-12)
        k = k * (1.0 + (a - 1.0) * ka_ref[:, sl])
        avec = -kk
        bvec = kk * a
        lw_hi, lw_lo = _split(lw)
        cum = (jnp.dot(ltri, lw_hi, preferred_element_type=F32)
               + jnp.dot(ltri, lw_lo, preferred_element_type=F32))
        cmid = cum[L // 2 - 1:L // 2, :]
        clast = cum[L - 1:L, :]
        e_neg = jnp.exp(cmid - cum)
        e_last = jnp.exp(clast - cum)
        return dict(
            v=v, rkb=r * k * rk_ref[:, sl],
            p_last=jnp.exp(clast), p_mid=jnp.exp(cmid),
            r_h=r * jnp.exp(cum - cmid),
            a_h=avec * jnp.exp(cum - lw - cmid),
            b_t=bvec * e_neg, k_t=k * e_neg, b_d=bvec * e_last, k_d=k * e_last)

    states = [s_ref[p] for p in range(pairs)]

    def sbs(att, r0, c0):
        return jnp.concatenate([att[r0:r0 + L, c0:c0 + L], att[r0 + 2 * L:r0 + 3 * L, c0:c0 + L]], axis=1)

    def interactions(q):
        return _bdot_nt(both(jnp.concatenate([q["a_h"], q["r_h"]], axis=0)),
                        jnp.concatenate([q["b_t"], q["k_t"]], axis=0))

    half = max(pairs // 2, 1)
    pre = [prepare(sl) for sl in slabs[:half]]
    att = [interactions(q) for q in pre]
    pre += [prepare(sl) for sl in slabs[half:]]
    a_ab = [jnp.where(strict2, sbs(x, 0, 0), 0.0) for x in att]
    t_inv = _unit_lower_inverses(a_ab, eye2, xr2)
    att += [interactions(q) for q in pre[half:]]
    a_ab += [jnp.where(strict2, sbs(x, 0, 0), 0.0) for x in att[half:]]
    t_inv += _unit_lower_inverses(a_ab[half:], eye2, xr2)
    a_ak = [jnp.where(strict2, sbs(x, 0, L), 0.0) for x in att]
    a_rb = [jnp.where(lower2, sbs(x, L, 0), 0.0) for x in att]
    a_rk = [jnp.where(lower2, sbs(x, L, L), 0.0) for x in att]

    v2 = [both(q["v"]) for q in pre]
    akv = [_bdot(x, y) for x, y in zip(a_ak, v2)]
    wu = [_bdot(t, both(jnp.concatenate([q["a_h"], u], axis=1)))
          for t, u, q in zip(t_inv, akv, pre)]
    qy = [_bdot(x, both(y)) for x, y in zip(a_rb, wu)]
    yk = [_bdot(x, y) for x, y in zip(a_rk, v2)]

    q_t = [(q["r_h"] + x[:, :LANES]) * q["p_mid"] for q, x in zip(pre, qy)]
    y_in = [x[:, LANES:] + z for x, z in zip(qy, yk)]
    bdt = [q["b_d"].T for q in pre]
    kdt = [q["k_d"].T for q in pre]
    m2 = [eye * q["p_last"] + jnp.where(same_head, _bdot(b, x[:, :LANES] * q["p_mid"]), 0.0)
          for q, b, x in zip(pre, bdt, wu)]
    g2 = [jnp.where(same_head, _bdot(jnp.concatenate([b, kd], axis=1),
                                     jnp.concatenate([x[:, LANES:], q["v"]], axis=0)), 0.0)
          for q, b, kd, x in zip(pre, bdt, kdt, wu)]
    y = [_bdot(x, s) + z for x, s, z in zip(q_t, states, y_in)]
    new_states = [_bdot(m, s) + g for m, s, g in zip(m2, states, g2)]

    d = [x - segsum(x) * (1.0 / HEAD) for x in y]
    var = [segsum(x * x) * (1.0 / HEAD) for x in d]
    bonus = [segsum(q["rkb"]) * q["v"] for q in pre]
    for p, sl in enumerate(slabs):
        s_ref[p] = new_states[p]
        yn = d[p] * lax.rsqrt(var[p] + GN_EPS) * lg_ref[:, sl] + lb_ref[:, sl]
        o_ref[:, sl] = ((yn + bonus[p]) * g_ref[:, sl]).astype(o_ref.dtype)


def _wkv7(r, k, v, wl, al, g, w0, a0, k_k, k_a, r_k, lnx_g, lnx_b, vres=None):
    t, d = r.shape
    pairs = min(WKV_PAIRS, d // LANES)
    pw = pairs * LANES
    assert t % CHUNK == 0 and d % pw == 0
    seq = pl.BlockSpec((CHUNK, pw), lambda j, c: (c, j))
    par = pl.BlockSpec((1, pw), lambda j, c: (0, j))
    row = lambda a: a.reshape(1, d).astype(F32)
    seq_args = [r, k, v, wl, al, g]
    par_args = [row(w0), row(a0), row(k_k), row(k_a), row(r_k), row(lnx_g), row(lnx_b)]
    if vres is not None:
        v_first, vl, v0 = vres
        seq_args += [v_first, vl]
        par_args += [row(v0)]
    return pl.pallas_call(
        functools.partial(_wkv_kernel, pairs=pairs, has_vres=vres is not None),
        grid=(d // pw, t // CHUNK),
        in_specs=[seq] * len(seq_args) + [par] * len(par_args),
        out_specs=seq,
        out_shape=jax.ShapeDtypeStruct((t, d), BF16),
        scratch_shapes=[pltpu.VMEM((pairs, LANES, LANES), F32)],
        compiler_params=_cparams(("parallel", "arbitrary")),
        name="wkv7",
    )(*seq_args, *par_args)


def _swa_kernel(sink_ref, q_ref, kc_ref, kp_ref, vc_ref, vp_ref, o_ref, *, group):
    n = pl.program_id(0)
    W = WINDOW
    qi = lax.broadcasted_iota(jnp.int32, (2 * W, 2 * W), 0) % W
    kj = lax.broadcasted_iota(jnp.int32, (2 * W, 2 * W), 1)
    valid = (kj > qi) & (kj <= qi + W) & ((kj >= W) | (n > 0))
    top = lax.broadcasted_iota(jnp.int32, (2 * W, 1), 0) < W
    lane = lax.broadcasted_iota(jnp.int32, (W, LANES), 1)
    head0 = lane < HEAD
    zero = jnp.zeros((), BF16)
    for hk in range(SWA_KV_HEADS):
        ks = slice(hk * LANES, (hk + 1) * LANES)
        k2 = jnp.concatenate([kp_ref[:, ks], kc_ref[:, ks]], axis=0)
        v2 = jnp.concatenate([vp_ref[:, ks], vc_ref[:, ks]], axis=0)
        for i in range(group // 2):
            slab = hk * (group // 2) + i
            qs = q_ref[:, slab * LANES:(slab + 1) * LANES]
            q2 = jnp.concatenate([jnp.where(head0, qs, zero), jnp.where(head0, zero, qs)], axis=0)
            s = lax.dot_general(q2, k2, (((1,), (1,)), ((), ())), preferred_element_type=F32)
            s = jnp.where(valid, s, NEG_INF)
            sink = jnp.where(top, sink_ref[2 * slab], sink_ref[2 * slab + 1])
            m = jnp.maximum(jnp.max(s, axis=-1, keepdims=True), sink)
            e = jnp.exp(s - m)
            den = jnp.sum(e, axis=-1, keepdims=True) + jnp.exp(sink - m)
            pr = (e / den).astype(BF16)
            o = jnp.dot(pr, v2, preferred_element_type=F32)
            o_ref[:, slab * LANES:(slab + 1) * LANES] = jnp.where(head0, o[:W], o[W:]).astype(o_ref.dtype)


def _swa_attention(qkv, sinks, d):
    t = qkv.shape[0]
    kvw = SWA_KV_HEADS * LANES
    qb = d // kvw
    group = d // HEAD // SWA_KV_HEADS
    assert group % 2 == 0 and d % kvw == 0 and t % WINDOW == 0
    prev = lambda n: jnp.maximum(n - 1, 0)
    return pl.pallas_call(
        functools.partial(_swa_kernel, group=group),
        grid=(t // WINDOW,),
        in_specs=[pl.BlockSpec(memory_space=pltpu.SMEM),
                  pl.BlockSpec((WINDOW, d), lambda n: (n, 0)),
                  pl.BlockSpec((WINDOW, kvw), lambda n: (n, qb)),
                  pl.BlockSpec((WINDOW, kvw), lambda n: (prev(n), qb)),
                  pl.BlockSpec((WINDOW, kvw), lambda n: (n, qb + 1)),
                  pl.BlockSpec((WINDOW, kvw), lambda n: (prev(n), qb + 1))],
        out_specs=pl.BlockSpec((WINDOW, d), lambda n: (n, 0)),
        out_shape=jax.ShapeDtypeStruct((t, d), BF16),
        compiler_params=_cparams(("arbitrary",)),
        name="swa_attention",
    )(sinks.astype(F32), qkv, qkv, qkv, qkv, qkv)


def _conv_kernel(u_ref, halo_ref, w_ref, b_ref, g_ref, beta_ref, o_ref, buf_ref, acc_ref):
    tm, d = u_ref.shape
    first = pl.program_id(0) == 0
    buf_ref[0:CONV_HALO, :] = jnp.where(first, 0.0, halo_ref[...])
    buf_ref[CONV_HALO:, :] = u_ref[...]
    rb = min(tm, 64)
    base = CONV_HALO - (CONV_WIDTH - 1)

    def col_body(c, carry):
        cs = pl.ds(pl.multiple_of(c * LANES, LANES), LANES)
        for r0 in range(0, tm, rb):
            acc = None
            for s in range(SUBLANES):
                rows = rb if s == 0 else rb + SUBLANES
                part = None
                for o in range(base, base + CONV_WIDTH):
                    if o % SUBLANES != s:
                        continue
                    term = buf_ref[pl.ds(r0 + o - s, rows), cs] * w_ref[pl.ds(o - base, 1), cs]
                    part = term if part is None else part + term
                if s:
                    part = pltpu.roll(part, shift=rows - s, axis=0)[:rb]
                acc = part if acc is None else acc + part
            acc_ref[pl.ds(r0, rb), cs] = acc
        return carry

    lax.fori_loop(0, d // LANES, col_body, 0)
    y = acc_ref[...] + b_ref[...]
    mu = jnp.mean(y, axis=-1, keepdims=True)
    yc = y - mu
    var = jnp.mean(yc * yc, axis=-1, keepdims=True)
    yn = yc * lax.rsqrt(var + LN_EPS) * g_ref[...] + beta_ref[...]
    o_ref[...] = (yn * jax.nn.sigmoid(yn)).astype(o_ref.dtype)


def _conv_ln_silu(u, w_dw, b_dw, ln_g, ln_b):
    t, d = u.shape
    tm = _tile(t, 256)
    assert tm % CONV_HALO == 0
    ratio = tm // CONV_HALO
    row = lambda a: a.reshape(1, d).astype(F32)
    vec = pl.BlockSpec((1, d), lambda i: (0, 0))
    return pl.pallas_call(
        _conv_kernel,
        grid=(t // tm,),
        in_specs=[pl.BlockSpec((tm, d), lambda i: (i, 0)),
                  pl.BlockSpec((CONV_HALO, d), lambda i: (jnp.maximum(i * ratio - 1, 0), 0)),
                  pl.BlockSpec((CONV_WIDTH, d), lambda i: (0, 0)),
                  vec, vec, vec],
        out_specs=pl.BlockSpec((tm, d), lambda i: (i, 0)),
        out_shape=jax.ShapeDtypeStruct((t, d), BF16),
        scratch_shapes=[pltpu.VMEM((tm + CONV_HALO, d), F32), pltpu.VMEM((tm, d), F32)],
        compiler_params=_cparams(("arbitrary",)),
        name="conv_dw_ln_silu",
    )(u, u, w_dw.astype(F32), row(b_dw), row(ln_g), row(ln_b))


def _pad_to(a, axis, mult):
    pad = (-a.shape[axis]) % mult
    if pad == 0:
        return a
    widths = [(0, 0)] * a.ndim
    widths[axis] = (0, pad)
    return jnp.pad(a, widths)


def _rwkv_layer(x, norm_g, v_first, vres, mix, w_rkv, layer, w0, w1, w2, a0, a1, a2, g1, g2,
                k_k, k_a, r_k, lnx_g, lnx_b, w_o):
    d = x.shape[1]
    branches = [(5, g1, g2, "sigmoid"), (3, w1, w2, "tanh"), (4, a1, a2, None)]
    if vres is not None:
        branches.append((2, vres[1], vres[2], None))
    downs = [_pad_to(b[1], 1, LANES) for b in branches]
    widths = [w.shape[1] for w in downs]
    offs = [sum(widths[:i]) for i in range(len(widths))]
    assert all(o % w == 0 for o, w in zip(offs, widths))
    w_down = jnp.concatenate(downs, axis=1).astype(BF16)
    loras = tuple((b[0], o, w, b[3]) for b, o, w in zip(branches, offs, widths))
    xs, mids = _rwkv_mix(x, norm_g, mix, w_down, loras)
    r, k, v = (_matmul(xs, w_rkv, x_index=c, w_index=3 * layer + c, name="rwkv_" + "rkv"[c])
               for c in range(3))
    ups = [_matmul(mids, _pad_to(b[2], 0, LANES).astype(BF16), x_col=o // w, kd=w, name="rwkv_lora_up")
           for b, o, w in zip(branches, offs, widths)]
    g, wl, al = ups[:3]
    if vres is None:
        v_first = v
        vres_args = None
    else:
        vres_args = (v_first, ups[3], vres[0])
    z = _wkv7(r, k, v, wl, al, g, w0, a0, k_k, k_a, r_k.reshape(d), lnx_g, lnx_b, vres=vres_args)
    return _matmul(z, w_o, w_index=layer, residual=x, name="rwkv_out"), v_first


def _swa_layer(x, norm_g, w_qkv, b_qkv, sinks, w_o, layer, b_o):
    d = x.shape[1]
    kvw = SWA_KV_HEADS * HEAD
    scale = HEAD ** -0.5

    def dup(a):
        a = a.reshape(a.shape[:-1] + (SWA_KV_HEADS, 1, HEAD))
        a = jnp.broadcast_to(a, a.shape[:-2] + (2, HEAD))
        return a.reshape(a.shape[:-3] + (2 * kvw,))

    w_ext = jnp.concatenate([w_qkv[:, :d] * scale, dup(w_qkv[:, d:d + kvw]), dup(w_qkv[:, d + kvw:])], axis=1)
    b_ext = jnp.concatenate([b_qkv[:d] * scale, dup(b_qkv[d:d + kvw]), dup(b_qkv[d + kvw:])], axis=0)
    h = _rmsnorm(x, norm_g, BF16)
    qkv = _matmul(h, w_ext.astype(BF16), bias=b_ext, out_dtype=BF16, name="swa_qkv")
    o = _swa_attention(qkv, sinks, d)
    return _matmul(o, w_o, w_index=layer, bias=b_o, residual=x, name="swa_out")


def _conv_layer(x, norm_g, w_pw1, layer, b_pw1, w_dw, b_dw, ln_g, ln_b, w_pw2, b_pw2):
    h = _rmsnorm(x, norm_g, BF16)
    u = _glu_matmul(h, w_pw1, layer, b_pw1)
    u = _conv_ln_silu(u, w_dw, b_dw, ln_g, ln_b)
    return _matmul(u, w_pw2, w_index=layer, bias=b_pw2, residual=x, name="conv_pw2")


def _mlp(x, norm_g, w_in, w_out, layer):
    h = _rmsnorm(x, norm_g, BF16)
    a, w_out_bf16 = _matmul(h, w_in, w_index=layer, act="relu2", out_dtype=BF16,
                            side_cast=(w_out, layer), name="mlp_in")
    return _matmul(a, w_out_bf16, residual=x, tm=1024, tn=1024, tk=2048, name="mlp_out")


def kernel(x, norm_mix_g, norm_mlp_g, norm_f_g, rwkv_mix, rwkv_w_rkv, rwkv_w0, rwkv_w1, rwkv_w2, rwkv_a0, rwkv_a1, rwkv_a2, rwkv_v0, rwkv_v1, rwkv_v2, rwkv_g1, rwkv_g2, rwkv_k_k, rwkv_k_a, rwkv_r_k, rwkv_lnx_g, rwkv_lnx_b, rwkv_w_o, swa_w_qkv, swa_b_qkv, swa_sinks, swa_w_o, swa_b_o, conv_w_pw1, conv_b_pw1, conv_w_dw, conv_b_dw, conv_ln_g, conv_ln_b, conv_w_pw2, conv_b_pw2, mlp_w_in, mlp_w_out):
    b, t, d = x.shape
    assert b == 1, "token shift and the WKV scan are written for a single sequence"
    depth = norm_mix_g.shape[0]
    xs = x.reshape(t, d)
    w_rkv = rwkv_w_rkv.reshape((-1,) + rwkv_w_rkv.shape[2:])
    ia = ib = ic = 0
    v_first = None
    for i in range(depth):
        kind = i % 3
        if kind == 0:
            vres = None if ia == 0 else (rwkv_v0[ia - 1], rwkv_v1[ia - 1], rwkv_v2[ia - 1])
            xs, v_first = _rwkv_layer(xs, norm_mix_g[i], v_first, vres, rwkv_mix[ia], w_rkv, ia,
                                      rwkv_w0[ia], rwkv_w1[ia], rwkv_w2[ia], rwkv_a0[ia], rwkv_a1[ia],
                                      rwkv_a2[ia], rwkv_g1[ia], rwkv_g2[ia], rwkv_k_k[ia], rwkv_k_a[ia],
                                      rwkv_r_k[ia], rwkv_lnx_g[ia], rwkv_lnx_b[ia], rwkv_w_o)
            ia += 1
        elif kind == 1:
            xs = _swa_layer(xs, norm_mix_g[i], swa_w_qkv[ib], swa_b_qkv[ib], swa_sinks[ib],
                            swa_w_o, ib, swa_b_o[ib])
            ib += 1
        else:
            xs = _conv_layer(xs, norm_mix_g[i], conv_w_pw1, ic, conv_b_pw1[ic], conv_w_dw[ic],
                             conv_b_dw[ic], conv_ln_g[ic], conv_ln_b[ic], conv_w_pw2, conv_b_pw2[ic])
            ic += 1
        xs = _mlp(xs, norm_mlp_g[i], mlp_w_in, mlp_w_out, i)
    return _rmsnorm(xs, norm_f_g, F32).reshape(b, t, d)
```

```python
import functools

import jax
import jax.numpy as jnp
from jax import lax
from jax.experimental import pallas as pl
from jax.experimental.pallas import tpu as pltpu

F32 = jnp.float32
BF16 = jnp.bfloat16

LANES = 128
SUBLANES = 8
HEAD = 64
CHUNK = 128
WKV_PAIRS = 16
WINDOW = 128
SWA_KV_HEADS = 8
CONV_WIDTH = 31
CONV_HALO = 32
RMS_EPS = 1e-5
LN_EPS = 1e-5
GN_EPS = 64e-5
NEG_INF = -1e30
VMEM_LIMIT = 56 * 1024 * 1024


def _cparams(sem):
    return pltpu.CompilerParams(dimension_semantics=sem, vmem_limit_bytes=VMEM_LIMIT)


def _tile(n, pref):
    if n <= pref:
        return n
    t = pref
    while n % t:
        t //= 2
    return t


def _rmsnorm_kernel(x_ref, g_ref, o_ref):
    x = x_ref[...]
    y = x * lax.rsqrt(jnp.mean(x * x, axis=-1, keepdims=True) + RMS_EPS)
    o_ref[...] = (y * g_ref[...]).astype(o_ref.dtype)


def _rmsnorm(x, g, out_dtype):
    m, d = x.shape
    tm = _tile(m, 256)
    return pl.pallas_call(
        _rmsnorm_kernel,
        grid=(m // tm,),
        in_specs=[pl.BlockSpec((tm, d), lambda i: (i, 0)),
                  pl.BlockSpec((1, d), lambda i: (0, 0))],
        out_specs=pl.BlockSpec((tm, d), lambda i: (i, 0)),
        out_shape=jax.ShapeDtypeStruct((m, d), out_dtype),
        compiler_params=_cparams(("arbitrary",)),
        name="rmsnorm",
    )(x, g.reshape(1, d))


def _rwkv_mix_kernel(x_ref, g_ref, mix_ref, wd_ref, o_ref, mid_ref, carry_ref, *, loras):
    @pl.when(pl.program_id(0) == 0)
    def _():
        carry_ref[...] = jnp.zeros_like(carry_ref)

    x = x_ref[...]
    h = x * lax.rsqrt(jnp.mean(x * x, axis=-1, keepdims=True) + RMS_EPS) * g_ref[...]
    tm = h.shape[0]
    prev = pltpu.roll(h, shift=1, axis=0)
    row = lax.broadcasted_iota(jnp.int32, h.shape, 0)
    prev = jnp.where(row == 0, carry_ref[...], prev)
    carry_ref[...] = h[tm - 1:tm, :]
    xx = prev - h
    streams = {}

    def stream(c):
        if c not in streams:
            streams[c] = (h + xx * mix_ref[c:c + 1, :]).astype(BF16)
        return streams[c]

    for c in range(3):
        o_ref[c] = stream(c)
    for c, off, width, act in loras:
        mid = jnp.dot(stream(c), wd_ref[:, off:off + width], preferred_element_type=F32)
        mid_ref[:, off:off + width] = _activate(mid, act).astype(mid_ref.dtype)


def _rwkv_mix(x, g, mix, w_down, loras):
    m, d = x.shape
    nl = w_down.shape[1]
    tm = _tile(m, 256)
    return pl.pallas_call(
        functools.partial(_rwkv_mix_kernel, loras=loras),
        grid=(m // tm,),
        in_specs=[pl.BlockSpec((tm, d), lambda i: (i, 0)),
                  pl.BlockSpec((1, d), lambda i: (0, 0)),
                  pl.BlockSpec((6, d), lambda i: (0, 0)),
                  pl.BlockSpec((d, nl), lambda i: (0, 0))],
        out_specs=[pl.BlockSpec((3, tm, d), lambda i: (0, i, 0)),
                   pl.BlockSpec((tm, nl), lambda i: (i, 0))],
        out_shape=[jax.ShapeDtypeStruct((3, m, d), BF16), jax.ShapeDtypeStruct((m, nl), BF16)],
        scratch_shapes=[pltpu.VMEM((1, d), F32)],
        compiler_params=_cparams(("arbitrary",)),
        name="rwkv_mix",
    )(x, g.reshape(1, d), mix, w_down)


def _activate(acc, act):
    if act is None:
        return acc
    if act == "tanh":
        return jnp.tanh(acc)
    if act == "sigmoid":
        return jax.nn.sigmoid(acc)
    if act == "relu2":
        return jnp.square(jnp.maximum(acc, 0.0))
    raise ValueError(act)


def _mm_kernel(*refs, nk, act, has_bias, has_res, cast_w, has_side):
    x_ref, w_ref = refs[0], refs[1]
    pos = 2
    b_ref = r_ref = None
    if has_bias:
        b_ref = refs[pos]
        pos += 1
    if has_res:
        r_ref = refs[pos]
        pos += 1
    if has_side:
        side_in_ref, side_out_ref = refs[pos], refs[pos + 2]
        side_out_ref[...] = side_in_ref[...].astype(side_out_ref.dtype)
        o_ref = refs[pos + 1]
        pos += 2
    else:
        o_ref = refs[pos]
    scratch = list(refs[pos + 1:])

    def epilogue(acc):
        if has_bias:
            acc = acc + b_ref[...]
        acc = _activate(acc, act)
        if has_res:
            acc = acc + r_ref[...]
        o_ref[...] = acc.astype(o_ref.dtype)

    if cast_w:
        wb_ref = scratch.pop(0)

        @pl.when(pl.program_id(1) == 0)
        def _():
            wb_ref[...] = w_ref[...].astype(BF16)

        w = wb_ref[...]
    else:
        w = w_ref[...]
    part = jnp.dot(x_ref[...], w, preferred_element_type=F32)
    if nk == 1:
        epilogue(part)
        return
    acc_ref = scratch.pop(0)
    k = pl.program_id(2)

    @pl.when(k == 0)
    def _():
        acc_ref[...] = part

    @pl.when(k > 0)
    def _():
        acc_ref[...] += part

    @pl.when(k == nk - 1)
    def _():
        epilogue(acc_ref[...])


def _matmul(x, w, *, name, x_index=None, x_col=0, kd=None, w_index=None, bias=None, act=None,
            residual=None, out_dtype=F32, tm=None, tn=None, tk=4096, side_cast=None):
    m = x.shape[-2]
    kd = x.shape[-1] if kd is None else kd
    n = w.shape[-1]
    assert w.shape[-2] == kd
    cast_w = w.dtype != BF16
    if tn is None:
        tn = 512 if cast_w else 1024
    if tm is None:
        tm = 1024 if cast_w else 512
    tm, tn, tk = _tile(m, tm), _tile(n, tn), _tile(kd, tk)
    nk = kd // tk
    assert not cast_w or nk == 1, "the cast-once path needs the weight block fixed across row tiles"
    assert x_col == 0 or nk == 1
    if w.ndim == 3:
        w_spec = pl.BlockSpec((None, tk, tn), lambda j, i, k: (w_index, k, j))
    else:
        w_spec = pl.BlockSpec((tk, tn), lambda j, i, k: (k, j))
    if x.ndim == 3:
        x_spec = pl.BlockSpec((None, tm, tk), lambda j, i, k: (x_index, i, k + x_col))
    else:
        x_spec = pl.BlockSpec((tm, tk), lambda j, i, k: (i, k + x_col))
    in_specs = [x_spec, w_spec]
    args = [x, w]
    if bias is not None:
        in_specs.append(pl.BlockSpec((1, tn), lambda j, i, k: (0, j)))
        args.append(bias.reshape(1, n).astype(F32))
    if residual is not None:
        in_specs.append(pl.BlockSpec((tm, tn), lambda j, i, k: (i, j)))
        args.append(residual)
    grid = (n // tn, m // tm, nk)
    out_specs = pl.BlockSpec((tm, tn), lambda j, i, k: (i, j))
    out_shape = jax.ShapeDtypeStruct((m, n), out_dtype)
    if side_cast is not None:
        stack, s_index = side_cast
        rows, cols = stack.shape[-2:]
        steps = grid[0] * grid[1]
        assert nk == 1 and rows % (steps * 16) == 0
        slab = rows // steps
        in_specs.append(pl.BlockSpec((None, slab, cols), lambda j, i, k: (s_index, j * grid[1] + i, 0)))
        args.append(stack)
        out_specs = [out_specs, pl.BlockSpec((slab, cols), lambda j, i, k: (j * grid[1] + i, 0))]
        out_shape = [out_shape, jax.ShapeDtypeStruct((rows, cols), BF16)]
    scratch = []
    if cast_w:
        scratch.append(pltpu.VMEM((tk, tn), BF16))
    if nk > 1:
        scratch.append(pltpu.VMEM((tm, tn), F32))
    return pl.pallas_call(
        functools.partial(_mm_kernel, nk=nk, act=act, has_bias=bias is not None,
                          has_res=residual is not None, cast_w=cast_w, has_side=side_cast is not None),
        grid=grid,
        in_specs=in_specs,
        out_specs=out_specs,
        out_shape=out_shape,
        scratch_shapes=scratch,
        compiler_params=_cparams(("arbitrary" if cast_w else "parallel",) * 2 + ("arbitrary",)),
        name=name,
    )(*args)


def _glu_mm_kernel(x_ref, wa_ref, wb_ref, ba_ref, bb_ref, o_ref, wab_ref, wbb_ref):
    @pl.when(pl.program_id(1) == 0)
    def _():
        wab_ref[...] = wa_ref[...].astype(BF16)
        wbb_ref[...] = wb_ref[...].astype(BF16)

    x = x_ref[...]
    a = jnp.dot(x, wab_ref[...], preferred_element_type=F32) + ba_ref[...]
    b = jnp.dot(x, wbb_ref[...], preferred_element_type=F32) + bb_ref[...]
    o_ref[...] = (a * jax.nn.sigmoid(b)).astype(o_ref.dtype)


def _glu_matmul(x, w, w_index, bias):
    m, kd = x.shape
    n = w.shape[-1] // 2
    tm, tn = _tile(m, 512), _tile(n, 256)
    nj = n // tn
    b2 = bias.reshape(1, 2 * n).astype(F32)
    return pl.pallas_call(
        _glu_mm_kernel,
        grid=(nj, m // tm),
        in_specs=[pl.BlockSpec((tm, kd), lambda j, i: (i, 0)),
                  pl.BlockSpec((None, kd, tn), lambda j, i: (w_index, 0, j)),
                  pl.BlockSpec((None, kd, tn), lambda j, i: (w_index, 0, j + nj)),
                  pl.BlockSpec((1, tn), lambda j, i: (0, j)),
                  pl.BlockSpec((1, tn), lambda j, i: (0, j + nj))],
        out_specs=pl.BlockSpec((tm, tn), lambda j, i: (i, j)),
        out_shape=jax.ShapeDtypeStruct((m, n), F32),
        scratch_shapes=[pltpu.VMEM((kd, tn), BF16), pltpu.VMEM((kd, tn), BF16)],
        compiler_params=_cparams(("arbitrary", "arbitrary")),
        name="conv_pw1_glu",
    )(x, w, w, b2, b2)


def _bdot(a, b):
    return jnp.dot(a.astype(BF16), b.astype(BF16), preferred_element_type=F32)


def _bdot_nt(a, b):
    return lax.dot_general(a.astype(BF16), b.astype(BF16), (((1,), (1,)), ((), ())),
                           preferred_element_type=F32)


def _split(a):
    hi = a.astype(BF16)
    lo = (a - hi.astype(F32)).astype(BF16)
    return hi, lo


def _blockdiag(y):
    left = lax.broadcasted_iota(jnp.int32, y.shape, 1) < y.shape[0]
    return jnp.concatenate([jnp.where(left, y, 0.0), jnp.where(left, 0.0, y)], axis=0)


def _hdot(x, y):
    return _bdot(x, _blockdiag(y))


def _unit_lower_inverses(n_list, eye2, xr2):
    n8 = [jnp.where(xr2 < 8, n, 0.0) for n in n_list]
    n8_2 = [_hdot(x, x) for x in n8]
    n8_4 = [_hdot(x, x) for x in n8_2]
    n8_3 = [_hdot(x, y) for x, y in zip(n8, n8_2)]
    t = [eye2 + a + b + c for a, b, c in zip(n8, n8_2, n8_3)]
    t = [x + _hdot(x, y) for x, y in zip(t, n8_4)]
    k = 8
    while k < CHUNK:
        level = (xr2 >= k) & (xr2 < 2 * k)
        u = [_hdot(jnp.where(level, n, 0.0), x) for n, x in zip(n_list, t)]
        t = [x + _hdot(x, y) for x, y in zip(t, u)]
        k *= 2
    return t


def _wkv_kernel(*refs, pairs, has_vres):
    if has_vres:
        (r_ref, k_ref, v_ref, wl_ref, al_ref, g_ref, vf_ref, vl_ref,
         w0_ref, a0_ref, kk_ref, ka_ref, rk_ref, lg_ref, lb_ref, v0_ref, o_ref, s_ref) = refs
    else:
        (r_ref, k_ref, v_ref, wl_ref, al_ref, g_ref,
         w0_ref, a0_ref, kk_ref, ka_ref, rk_ref, lg_ref, lb_ref, o_ref, s_ref) = refs
        vf_ref = vl_ref = v0_ref = None

    @pl.when(pl.program_id(1) == 0)
    def _():
        s_ref[...] = jnp.zeros_like(s_ref)

    L = CHUNK
    row = lax.broadcasted_iota(jnp.int32, (L, L), 0)
    col = lax.broadcasted_iota(jnp.int32, (L, L), 1)
    lower = row >= col
    same_head = (row >= HEAD) == (col >= HEAD)
    eye = (row == col).astype(F32)
    ltri = lower.astype(BF16)
    seg = same_head.astype(BF16)
    row2 = lax.broadcasted_iota(jnp.int32, (L, 2 * L), 0)
    col2 = lax.broadcasted_iota(jnp.int32, (L, 2 * L), 1) % L
    lower2 = row2 >= col2
    strict2 = row2 > col2
    xr2 = row2 ^ col2
    eye2 = (row2 == col2).astype(F32)

    def both(a):
        h0 = lax.broadcasted_iota(jnp.int32, a.shape, 1) % LANES < HEAD
        return jnp.concatenate([jnp.where(h0, a, 0.0), jnp.where(h0, 0.0, a)], axis=0)

    slabs = [slice(p * LANES, (p + 1) * LANES) for p in range(pairs)]

    def segsum(x):
        return _bdot(x, seg)

    def prepare(sl):
        r = r_ref[:, sl]
        k = k_ref[:, sl]
        v = v_ref[:, sl]
        z = w0_ref[:, sl] + wl_ref[:, sl]
        w = jnp.minimum(z, 0.0) - jnp.log(1.0 + jnp.exp(-jnp.abs(z))) - 0.5
        lw = -jnp.exp(w)
        a = jax.nn.sigmoid(a0_ref[:, sl] + al_ref[:, sl])
        if has_vres:
            v = v + (vf_ref[:, sl] - v) * jax.nn.sigmoid(v0_ref[:, sl] + vl_ref[:, sl])
        kk = k * kk_ref[:, sl]
        kk = kk / jnp.maximum(jnp.sqrt(segsum(kk * kk)), 1e-12)
        k = k * (1.0 + (a - 1.0) * ka_ref[:, sl])
        avec = -kk
        bvec = kk * a
        lw_hi, lw_lo = _split(lw)
        cum = (jnp.dot(ltri, lw_hi, preferred_element_type=F32)
               + jnp.dot(ltri, lw_lo, preferred_element_type=F32))
        cmid = cum[L // 2 - 1:L // 2, :]
        clast = cum[L - 1:L, :]
        e_neg = jnp.exp(cmid - cum)
        e_last = jnp.exp(clast - cum)
        return dict(
            v=v, rkb=r * k * rk_ref[:, sl],
            p_last=jnp.exp(clast), p_mid=jnp.exp(cmid),
            r_h=r * jnp.exp(cum - cmid),
            a_h=avec * jnp.exp(cum - lw - cmid),
            b_t=bvec * e_neg, k_t=k * e_neg, b_d=bvec * e_last, k_d=k * e_last)

    states = [s_ref[p] for p in range(pairs)]

    def sbs(att, r0, c0):
        return jnp.concatenate([att[r0:r0 + L, c0:c0 + L], att[r0 + 2 * L:r0 + 3 * L, c0:c0 + L]], axis=1)

    def interactions(q):
        return _bdot_nt(both(jnp.concatenate([q["a_h"], q["r_h"]], axis=0)),
                        jnp.concatenate([q["b_t"], q["k_t"]], axis=0))

    half = max(pairs // 2, 1)
    pre = [prepare(sl) for sl in slabs[:half]]
    att = [interactions(q) for q in pre]
    pre += [prepare(sl) for sl in slabs[half:]]
    a_ab = [jnp.where(strict2, sbs(x, 0, 0), 0.0) for x in att]
    t_inv = _unit_lower_inverses(a_ab, eye2, xr2)
    att += [interactions(q) for q in pre[half:]]
    a_ab += [jnp.where(strict2, sbs(x, 0, 0), 0.0) for x in att[half:]]
    t_inv += _unit_lower_inverses(a_ab[half:], eye2, xr2)
    a_ak = [jnp.where(strict2, sbs(x, 0, L), 0.0) for x in att]
    a_rb = [jnp.where(lower2, sbs(x, L, 0), 0.0) for x in att]
    a_rk = [jnp.where(lower2, sbs(x, L, L), 0.0) for x in att]

    v2 = [both(q["v"]) for q in pre]
    akv = [_bdot(x, y) for x, y in zip(a_ak, v2)]
    wu = [_bdot(t, both(jnp.concatenate([q["a_h"], u], axis=1)))
          for t, u, q in zip(t_inv, akv, pre)]
    qy = [_bdot(x, both(y)) for x, y in zip(a_rb, wu)]
    yk = [_bdot(x, y) for x, y in zip(a_rk, v2)]

    q_t = [(q["r_h"] + x[:, :LANES]) * q["p_mid"] for q, x in zip(pre, qy)]
    y_in = [x[:, LANES:] + z for x, z in zip(qy, yk)]
    bdt = [q["b_d"].T for q in pre]
    kdt = [q["k_d"].T for q in pre]
    m2 = [eye * q["p_last"] + jnp.where(same_head, _bdot(b, x[:, :LANES] * q["p_mid"]), 0.0)
          for q, b, x in zip(pre, bdt, wu)]
    g2 = [jnp.where(same_head, _bdot(jnp.concatenate([b, kd], axis=1),
                                     jnp.concatenate([x[:, LANES:], q["v"]], axis=0)), 0.0)
          for q, b, kd, x in zip(pre, bdt, kdt, wu)]
    y = [_bdot(x, s) + z for x, s, z in zip(q_t, states, y_in)]
    new_states = [_bdot(m, s) + g for m, s, g in zip(m2, states, g2)]

    d = [x - segsum(x) * (1.0 / HEAD) for x in y]
    var = [segsum(x * x) * (1.0 / HEAD) for x in d]
    bonus = [segsum(q["rkb"]) * q["v"] for q in pre]
    for p, sl in enumerate(slabs):
        s_ref[p] = new_states[p]
        yn = d[p] * lax.rsqrt(var[p] + GN_EPS) * lg_ref[:, sl] + lb_ref[:, sl]
        o_ref[:, sl] = ((yn + bonus[p]) * g_ref[:, sl]).astype(o_ref.dtype)


def _wkv7(r, k, v, wl, al, g, w0, a0, k_k, k_a, r_k, lnx_g, lnx_b, vres=None):
    t, d = r.shape
    pairs = min(WKV_PAIRS, d // LANES)
    pw = pairs * LANES
    assert t % CHUNK == 0 and d % pw == 0
    seq = pl.BlockSpec((CHUNK, pw), lambda j, c: (c, j))
    par = pl.BlockSpec((1, pw), lambda j, c: (0, j))
    row = lambda a: a.reshape(1, d).astype(F32)
    seq_args = [r, k, v, wl, al, g]
    par_args = [row(w0), row(a0), row(k_k), row(k_a), row(r_k), row(lnx_g), row(lnx_b)]
    if vres is not None:
        v_first, vl, v0 = vres
        seq_args += [v_first, vl]
        par_args += [row(v0)]
    return pl.pallas_call(
        functools.partial(_wkv_kernel, pairs=pairs, has_vres=vres is not None),
        grid=(d // pw, t // CHUNK),
        in_specs=[seq] * len(seq_args) + [par] * len(par_args),
        out_specs=seq,
        out_shape=jax.ShapeDtypeStruct((t, d), BF16),
        scratch_shapes=[pltpu.VMEM((pairs, LANES, LANES), F32)],
        compiler_params=_cparams(("parallel", "arbitrary")),
        name="wkv7",
    )(*seq_args, *par_args)


def _swa_kernel(sink_ref, q_ref, kc_ref, kp_ref, vc_ref, vp_ref, o_ref, *, group):
    n = pl.program_id(0)
    W = WINDOW
    qi = lax.broadcasted_iota(jnp.int32, (2 * W, 2 * W), 0) % W
    kj = lax.broadcasted_iota(jnp.int32, (2 * W, 2 * W), 1)
    valid = (kj > qi) & (kj <= qi + W) & ((kj >= W) | (n > 0))
    top = lax.broadcasted_iota(jnp.int32, (2 * W, 1), 0) < W
    lane = lax.broadcasted_iota(jnp.int32, (W, LANES), 1)
    head0 = lane < HEAD
    zero = jnp.zeros((), BF16)
    for hk in range(SWA_KV_HEADS):
        ks = slice(hk * LANES, (hk + 1) * LANES)
        k2 = jnp.concatenate([kp_ref[:, ks], kc_ref[:, ks]], axis=0)
        v2 = jnp.concatenate([vp_ref[:, ks], vc_ref[:, ks]], axis=0)
        for i in range(group // 2):
            slab = hk * (group // 2) + i
            qs = q_ref[:, slab * LANES:(slab + 1) * LANES]
            q2 = jnp.concatenate([jnp.where(head0, qs, zero), jnp.where(head0, zero, qs)], axis=0)
            s = lax.dot_general(q2, k2, (((1,), (1,)), ((), ())), preferred_element_type=F32)
            s = jnp.where(valid, s, NEG_INF)
            sink = jnp.where(top, sink_ref[2 * slab], sink_ref[2 * slab + 1])
            m = jnp.maximum(jnp.max(s, axis=-1, keepdims=True), sink)
            e = jnp.exp(s - m)
            den = jnp.sum(e, axis=-1, keepdims=True) + jnp.exp(sink - m)
            pr = (e / den).astype(BF16)
            o = jnp.dot(pr, v2, preferred_element_type=F32)
            o_ref[:, slab * LANES:(slab + 1) * LANES] = jnp.where(head0, o[:W], o[W:]).astype(o_ref.dtype)


def _swa_attention(qkv, sinks, d):
    t = qkv.shape[0]
    kvw = SWA_KV_HEADS * LANES
    qb = d // kvw
    group = d // HEAD // SWA_KV_HEADS
    assert group % 2 == 0 and d % kvw == 0 and t % WINDOW == 0
    prev = lambda n: jnp.maximum(n - 1, 0)
    return pl.pallas_call(
        functools.partial(_swa_kernel, group=group),
        grid=(t // WINDOW,),
        in_specs=[pl.BlockSpec(memory_space=pltpu.SMEM),
                  pl.BlockSpec((WINDOW, d), lambda n: (n, 0)),
                  pl.BlockSpec((WINDOW, kvw), lambda n: (n, qb)),
                  pl.BlockSpec((WINDOW, kvw), lambda n: (prev(n), qb)),
                  pl.BlockSpec((WINDOW, kvw), lambda n: (n, qb + 1)),
                  pl.BlockSpec((WINDOW, kvw), lambda n: (prev(n), qb + 1))],
        out_specs=pl.BlockSpec((WINDOW, d), lambda n: (n, 0)),
        out_shape=jax.ShapeDtypeStruct((t, d), BF16),
        compiler_params=_cparams(("arbitrary",)),
        name="swa_attention",
    )(sinks.astype(F32), qkv, qkv, qkv, qkv, qkv)


def _conv_kernel(u_ref, halo_ref, w_ref, b_ref, g_ref, beta_ref, o_ref, buf_ref, acc_ref):
    tm, d = u_ref.shape
    first = pl.program_id(0) == 0
    buf_ref[0:CONV_HALO, :] = jnp.where(first, 0.0, halo_ref[...])
    buf_ref[CONV_HALO:, :] = u_ref[...]
    rb = min(tm, 64)
    base = CONV_HALO - (CONV_WIDTH - 1)

    def col_body(c, carry):
        cs = pl.ds(pl.multiple_of(c * LANES, LANES), LANES)
        for r0 in range(0, tm, rb):
            acc = None
            for s in range(SUBLANES):
                rows = rb if s == 0 else rb + SUBLANES
                part = None
                for o in range(base, base + CONV_WIDTH):
                    if o % SUBLANES != s:
                        continue
                    term = buf_ref[pl.ds(r0 + o - s, rows), cs] * w_ref[pl.ds(o - base, 1), cs]
                    part = term if part is None else part + term
                if s:
                    part = pltpu.roll(part, shift=rows - s, axis=0)[:rb]
                acc = part if acc is None else acc + part
            acc_ref[pl.ds(r0, rb), cs] = acc
        return carry

    lax.fori_loop(0, d // LANES, col_body, 0)
    y = acc_ref[...] + b_ref[...]
    mu = jnp.mean(y, axis=-1, keepdims=True)
    yc = y - mu
    var = jnp.mean(yc * yc, axis=-1, keepdims=True)
    yn = yc * lax.rsqrt(var + LN_EPS) * g_ref[...] + beta_ref[...]
    o_ref[...] = (yn * jax.nn.sigmoid(yn)).astype(o_ref.dtype)


def _conv_ln_silu(u, w_dw, b_dw, ln_g, ln_b):
    t, d = u.shape
    tm = _tile(t, 256)
    assert tm % CONV_HALO == 0
    ratio = tm // CONV_HALO
    row = lambda a: a.reshape(1, d).astype(F32)
    vec = pl.BlockSpec((1, d), lambda i: (0, 0))
    return pl.pallas_call(
        _conv_kernel,
        grid=(t // tm,),
        in_specs=[pl.BlockSpec((tm, d), lambda i: (i, 0)),
                  pl.BlockSpec((CONV_HALO, d), lambda i: (jnp.maximum(i * ratio - 1, 0), 0)),
                  pl.BlockSpec((CONV_WIDTH, d), lambda i: (0, 0)),
                  vec, vec, vec],
        out_specs=pl.BlockSpec((tm, d), lambda i: (i, 0)),
        out_shape=jax.ShapeDtypeStruct((t, d), BF16),
        scratch_shapes=[pltpu.VMEM((tm + CONV_HALO, d), F32), pltpu.VMEM((tm, d), F32)],
        compiler_params=_cparams(("arbitrary",)),
        name="conv_dw_ln_silu",
    )(u, u, w_dw.astype(F32), row(b_dw), row(ln_g), row(ln_b))


def _pad_to(a, axis, mult):
    pad = (-a.shape[axis]) % mult
    if pad == 0:
        return a
    widths = [(0, 0)] * a.ndim
    widths[axis] = (0, pad)
    return jnp.pad(a, widths)


def _rwkv_layer(x, norm_g, v_first, vres, mix, w_rkv, layer, w0, w1, w2, a0, a1, a2, g1, g2,
                k_k, k_a, r_k, lnx_g, lnx_b, w_o):
    d = x.shape[1]
    branches = [(5, g1, g2, "sigmoid"), (3, w1, w2, "tanh"), (4, a1, a2, None)]
    if vres is not None:
        branches.append((2, vres[1], vres[2], None))
    downs = [_pad_to(b[1], 1, LANES) for b in branches]
    widths = [w.shape[1] for w in downs]
    offs = [sum(widths[:i]) for i in range(len(widths))]
    assert all(o % w == 0 for o, w in zip(offs, widths))
    w_down = jnp.concatenate(downs, axis=1).astype(BF16)
    loras = tuple((b[0], o, w, b[3]) for b, o, w in zip(branches, offs, widths))
    xs, mids = _rwkv_mix(x, norm_g, mix, w_down, loras)
    if layer == 0:
        r, k, v = (_matmul(xs, w_rkv, x_index=c, w_index=3 * layer + c, name="rwkv_" + "rkv"[c])
                   for c in range(3))
    else:
        r, k, v = (_matmul(xs, w_rkv[3 * layer + c].astype(BF16), x_index=c, name="rwkv_bf16_" + "rkv"[c])
                   for c in range(3))
    ups = [_matmul(mids, _pad_to(b[2], 0, LANES).astype(BF16), x_col=o // w, kd=w,
                   out_dtype=F32 if b[0] == 3 else BF16, name="rwkv_lora_up")
           for b, o, w in zip(branches, offs, widths)]
    g, wl, al = ups[:3]
    if vres is None:
        v_first = v
        vres_args = None
    else:
        vres_args = (v_first, ups[3], vres[0])
    z = _wkv7(r, k, v, wl, al, g, w0, a0, k_k, k_a, r_k.reshape(d), lnx_g, lnx_b, vres=vres_args)
    return _matmul(z, w_o, w_index=layer, residual=x, name="rwkv_out"), v_first


def _swa_layer(x, norm_g, w_qkv, b_qkv, sinks, w_o, layer, b_o):
    d = x.shape[1]
    kvw = SWA_KV_HEADS * HEAD
    scale = HEAD ** -0.5

    def dup(a):
        a = a.reshape(a.shape[:-1] + (SWA_KV_HEADS, 1, HEAD))
        a = jnp.broadcast_to(a, a.shape[:-2] + (2, HEAD))
        return a.reshape(a.shape[:-3] + (2 * kvw,))

    w_ext = jnp.concatenate([w_qkv[:, :d] * scale, dup(w_qkv[:, d:d + kvw]), dup(w_qkv[:, d + kvw:])], axis=1)
    b_ext = jnp.concatenate([b_qkv[:d] * scale, dup(b_qkv[d:d + kvw]), dup(b_qkv[d + kvw:])], axis=0)
    h = _rmsnorm(x, norm_g, BF16)
    qkv = _matmul(h, w_ext.astype(BF16), bias=b_ext, out_dtype=BF16, name="swa_qkv")
    o = _swa_attention(qkv, sinks, d)
    return _matmul(o, w_o, w_index=layer, bias=b_o, residual=x, name="swa_out")


def _conv_layer(x, norm_g, w_pw1, layer, b_pw1, w_dw, b_dw, ln_g, ln_b, w_pw2, b_pw2):
    h = _rmsnorm(x, norm_g, BF16)
    u = _glu_matmul(h, w_pw1, layer, b_pw1)
    u = _conv_ln_silu(u, w_dw, b_dw, ln_g, ln_b)
    return _matmul(u, w_pw2, w_index=layer, bias=b_pw2, residual=x, name="conv_pw2")


def _mlp(x, norm_g, w_in, w_out, layer):
    h = _rmsnorm(x, norm_g, BF16)
    if layer == 0:
        a, w_out_bf16 = _matmul(h, w_in, w_index=layer, act="relu2", out_dtype=BF16,
                                side_cast=(w_out, layer), name="mlp_in_cast_1024x512")
    elif layer == 1:
        a = _matmul(h, w_in, w_index=layer, act="relu2", out_dtype=BF16, name="mlp_in_cast_noside")
        w_out_bf16 = w_out[layer].astype(BF16)
    elif layer == 2:
        a, w_out_bf16 = _matmul(h, w_in[layer].astype(BF16), act="relu2", out_dtype=BF16, tm=512, tn=1024,
                                side_cast=(w_out, layer), name="mlp_in_bf16_512x1024")
    else:
        a, w_out_bf16 = _matmul(h, w_in[layer].astype(BF16), act="relu2", out_dtype=BF16, tm=1024, tn=1024,
                                side_cast=(w_out, layer), name="mlp_in_bf16_1024x1024")
    tiles = [(1024, 1024, 2048), (512, 1024, 4096), (1024, 1024, 512), (1024, 1024, 1024)][layer]
    return _matmul(a, w_out_bf16, residual=x, tm=tiles[0], tn=tiles[1], tk=tiles[2],
                   name="mlp_out_%dx%dx%d" % tiles)


def kernel(x, norm_mix_g, norm_mlp_g, norm_f_g, rwkv_mix, rwkv_w_rkv, rwkv_w0, rwkv_w1, rwkv_w2, rwkv_a0, rwkv_a1, rwkv_a2, rwkv_v0, rwkv_v1, rwkv_v2, rwkv_g1, rwkv_g2, rwkv_k_k, rwkv_k_a, rwkv_r_k, rwkv_lnx_g, rwkv_lnx_b, rwkv_w_o, swa_w_qkv, swa_b_qkv, swa_sinks, swa_w_o, swa_b_o, conv_w_pw1, conv_b_pw1, conv_w_dw, conv_b_dw, conv_ln_g, conv_ln_b, conv_w_pw2, conv_b_pw2, mlp_w_in, mlp_w_out):
    b, t, d = x.shape
    assert b == 1, "token shift and the WKV scan are written for a single sequence"
    depth = norm_mix_g.shape[0]
    xs = x.reshape(t, d)
    w_rkv = rwkv_w_rkv.reshape((-1,) + rwkv_w_rkv.shape[2:])
    ia = ib = ic = 0
    v_first = None
    for i in range(depth):
        kind = i % 3
        if kind == 0:
            vres = None if ia == 0 else (rwkv_v0[ia - 1], rwkv_v1[ia - 1], rwkv_v2[ia - 1])
            xs, v_first = _rwkv_layer(xs, norm_mix_g[i], v_first, vres, rwkv_mix[ia], w_rkv, ia,
                                      rwkv_w0[ia], rwkv_w1[ia], rwkv_w2[ia], rwkv_a0[ia], rwkv_a1[ia],
                                      rwkv_a2[ia], rwkv_g1[ia], rwkv_g2[ia], rwkv_k_k[ia], rwkv_k_a[ia],
                                      rwkv_r_k[ia], rwkv_lnx_g[ia], rwkv_lnx_b[ia], rwkv_w_o)
            ia += 1
        elif kind == 1:
            xs = _swa_layer(xs, norm_mix_g[i], swa_w_qkv[ib], swa_b_qkv[ib], swa_sinks[ib],
                            swa_w_o, ib, swa_b_o[ib])
            ib += 1
        else:
            xs = _conv_layer(xs, norm_mix_g[i], conv_w_pw1, ic, conv_b_pw1[ic], conv_w_dw[ic],
                             conv_b_dw[ic], conv_ln_g[ic], conv_ln_b[ic], conv_w_pw2, conv_b_pw2[ic])
            ic += 1
        xs = _mlp(xs, norm_mlp_g[i], mlp_w_in, mlp_w_out, i)
    return _rmsnorm(xs, norm_f_g, F32).reshape(b, t, d)
```

```python
import functools

import jax
import jax.numpy as jnp
from jax import lax
from jax.experimental import pallas as pl
from jax.experimental.pallas import tpu as pltpu

F32 = jnp.float32
BF16 = jnp.bfloat16

LANES = 128
SUBLANES = 8
HEAD = 64
CHUNK = 128
WKV_PAIRS = 16
WINDOW = 128
SWA_KV_HEADS = 8
CONV_WIDTH = 31
CONV_HALO = 32
RMS_EPS = 1e-5
LN_EPS = 1e-5
GN_EPS = 64e-5
NEG_INF = -1e30
VMEM_LIMIT = 56 * 1024 * 1024


def _cparams(sem):
    return pltpu.CompilerParams(dimension_semantics=sem, vmem_limit_bytes=VMEM_LIMIT)


def _tile(n, pref):
    if n <= pref:
        return n
    t = pref
    while n % t:
        t //= 2
    return t


def _rmsnorm_kernel(x_ref, g_ref, o_ref):
    x = x_ref[...]
    y = x * lax.rsqrt(jnp.mean(x * x, axis=-1, keepdims=True) + RMS_EPS)
    o_ref[...] = (y * g_ref[...]).astype(o_ref.dtype)


def _rmsnorm(x, g, out_dtype):
    m, d = x.shape
    tm = _tile(m, 256)
    return pl.pallas_call(
        _rmsnorm_kernel,
        grid=(m // tm,),
        in_specs=[pl.BlockSpec((tm, d), lambda i: (i, 0)),
                  pl.BlockSpec((1, d), lambda i: (0, 0))],
        out_specs=pl.BlockSpec((tm, d), lambda i: (i, 0)),
        out_shape=jax.ShapeDtypeStruct((m, d), out_dtype),
        compiler_params=_cparams(("arbitrary",)),
        name="rmsnorm",
    )(x, g.reshape(1, d))


def _rwkv_mix_kernel(x_ref, g_ref, mix_ref, wd_ref, o_ref, mid_ref, carry_ref, *, loras):
    @pl.when(pl.program_id(0) == 0)
    def _():
        carry_ref[...] = jnp.zeros_like(carry_ref)

    x = x_ref[...]
    h = x * lax.rsqrt(jnp.mean(x * x, axis=-1, keepdims=True) + RMS_EPS) * g_ref[...]
    tm = h.shape[0]
    prev = pltpu.roll(h, shift=1, axis=0)
    row = lax.broadcasted_iota(jnp.int32, h.shape, 0)
    prev = jnp.where(row == 0, carry_ref[...], prev)
    carry_ref[...] = h[tm - 1:tm, :]
    xx = prev - h
    streams = {}

    def stream(c):
        if c not in streams:
            streams[c] = (h + xx * mix_ref[c:c + 1, :]).astype(BF16)
        return streams[c]

    for c in range(3):
        o_ref[c] = stream(c)
    for c, off, width, act in loras:
        mid = jnp.dot(stream(c), wd_ref[:, off:off + width], preferred_element_type=F32)
        mid_ref[:, off:off + width] = _activate(mid, act).astype(mid_ref.dtype)


def _rwkv_mix(x, g, mix, w_down, loras):
    m, d = x.shape
    nl = w_down.shape[1]
    tm = _tile(m, 256)
    return pl.pallas_call(
        functools.partial(_rwkv_mix_kernel, loras=loras),
        grid=(m // tm,),
        in_specs=[pl.BlockSpec((tm, d), lambda i: (i, 0)),
                  pl.BlockSpec((1, d), lambda i: (0, 0)),
                  pl.BlockSpec((6, d), lambda i: (0, 0)),
                  pl.BlockSpec((d, nl), lambda i: (0, 0))],
        out_specs=[pl.BlockSpec((3, tm, d), lambda i: (0, i, 0)),
                   pl.BlockSpec((tm, nl), lambda i: (i, 0))],
        out_shape=[jax.ShapeDtypeStruct((3, m, d), BF16), jax.ShapeDtypeStruct((m, nl), BF16)],
        scratch_shapes=[pltpu.VMEM((1, d), F32)],
        compiler_params=_cparams(("arbitrary",)),
        name="rwkv_mix",
    )(x, g.reshape(1, d), mix, w_down)


def _activate(acc, act):
    if act is None:
        return acc
    if act == "tanh":
        return jnp.tanh(acc)
    if act == "sigmoid":
        return jax.nn.sigmoid(acc)
    if act == "relu2":
        return jnp.square(jnp.maximum(acc, 0.0))
    raise ValueError(act)


def _mm_kernel(*refs, nk, act, has_bias, has_res, cast_w, has_side):
    x_ref, w_ref = refs[0], refs[1]
    pos = 2
    b_ref = r_ref = None
    if has_bias:
        b_ref = refs[pos]
        pos += 1
    if has_res:
        r_ref = refs[pos]
        pos += 1
    if has_side:
        side_in_ref, side_out_ref = refs[pos], refs[pos + 2]
        side_out_ref[...] = side_in_ref[...].astype(side_out_ref.dtype)
        o_ref = refs[pos + 1]
        pos += 2
    else:
        o_ref = refs[pos]
    scratch = list(refs[pos + 1:])

    def epilogue(acc):
        if has_bias:
            acc = acc + b_ref[...]
        acc = _activate(acc, act)
        if has_res:
            acc = acc + r_ref[...]
        o_ref[...] = acc.astype(o_ref.dtype)

    if cast_w:
        wb_ref, stage_ref, sem = scratch
        layer, kc, tn, nj = cast_w
        j, i = pl.program_id(0), pl.program_id(1)

        def aligned(v, mult):
            return v if isinstance(v, int) else pl.multiple_of(v, mult)

        def chunk_copy(tile, c):
            rows = pl.ds(aligned(c * kc, kc), kc)
            cols = pl.ds(aligned(tile * tn, tn), tn)
            return pltpu.make_async_copy(w_ref.at[layer, rows, cols], stage_ref, sem), rows

        @pl.when((j == 0) & (i == 0))
        def _():
            for c in range(wb_ref.shape[1] // kc):
                copy, rows = chunk_copy(0, c)
                copy.start()
                copy.wait()
                wb_ref[0, rows, :] = stage_ref[...].astype(BF16)

        has_next = j + 1 < nj

        @pl.when(has_next)
        def _():
            chunk_copy(j + 1, i)[0].start()

        epilogue(jnp.dot(x_ref[...], wb_ref[j % 2], preferred_element_type=F32))

        @pl.when(has_next)
        def _():
            copy, rows = chunk_copy(j + 1, i)
            copy.wait()
            wb_ref[(j + 1) % 2, rows, :] = stage_ref[...].astype(BF16)

        return
    part = jnp.dot(x_ref[...], w_ref[...], preferred_element_type=F32)
    if nk == 1:
        epilogue(part)
        return
    acc_ref = scratch.pop(0)
    k = pl.program_id(2)

    @pl.when(k == 0)
    def _():
        acc_ref[...] = part

    @pl.when(k > 0)
    def _():
        acc_ref[...] += part

    @pl.when(k == nk - 1)
    def _():
        epilogue(acc_ref[...])


def _matmul(x, w, *, name, x_index=None, x_col=0, kd=None, w_index=None, bias=None, act=None,
            residual=None, out_dtype=F32, tm=None, tn=None, tk=4096, side_cast=None):
    m = x.shape[-2]
    kd = x.shape[-1] if kd is None else kd
    n = w.shape[-1]
    assert w.shape[-2] == kd
    stream_w = w.dtype != BF16
    if tn is None:
        tn = 1024
    if tm is None:
        tm = 1024 if stream_w and out_dtype == BF16 else 512
    tm, tn, tk = _tile(m, tm), _tile(n, tn), _tile(kd, tk)
    nk = kd // tk
    assert x_col == 0 or nk == 1
    cast_w = None
    if stream_w:
        ni = m // tm
        assert nk == 1 and w.ndim == 3 and kd % (ni * 16) == 0
        cast_w = (w_index, kd // ni, tn, n // tn)
        w_spec = pl.BlockSpec(memory_space=pl.ANY)
    elif w.ndim == 3:
        w_spec = pl.BlockSpec((None, tk, tn), lambda j, i, k: (w_index, k, j))
    else:
        w_spec = pl.BlockSpec((tk, tn), lambda j, i, k: (k, j))
    if x.ndim == 3:
        x_spec = pl.BlockSpec((None, tm, tk), lambda j, i, k: (x_index, i, k + x_col))
    else:
        x_spec = pl.BlockSpec((tm, tk), lambda j, i, k: (i, k + x_col))
    in_specs = [x_spec, w_spec]
    args = [x, w]
    if bias is not None:
        in_specs.append(pl.BlockSpec((1, tn), lambda j, i, k: (0, j)))
        args.append(bias.reshape(1, n).astype(F32))
    if residual is not None:
        in_specs.append(pl.BlockSpec((tm, tn), lambda j, i, k: (i, j)))
        args.append(residual)
    grid = (n // tn, m // tm, nk)
    out_specs = pl.BlockSpec((tm, tn), lambda j, i, k: (i, j))
    out_shape = jax.ShapeDtypeStruct((m, n), out_dtype)
    if side_cast is not None:
        stack, s_index = side_cast
        rows, cols = stack.shape[-2:]
        steps = grid[0] * grid[1]
        assert nk == 1 and rows % (steps * 16) == 0
        slab = rows // steps
        in_specs.append(pl.BlockSpec((None, slab, cols), lambda j, i, k: (s_index, j * grid[1] + i, 0)))
        args.append(stack)
        out_specs = [out_specs, pl.BlockSpec((slab, cols), lambda j, i, k: (j * grid[1] + i, 0))]
        out_shape = [out_shape, jax.ShapeDtypeStruct((rows, cols), BF16)]
    scratch = []
    if stream_w:
        scratch += [pltpu.VMEM((2, kd, tn), BF16), pltpu.VMEM((cast_w[1], tn), F32),
                    pltpu.SemaphoreType.DMA(())]
    if nk > 1:
        scratch.append(pltpu.VMEM((tm, tn), F32))
    return pl.pallas_call(
        functools.partial(_mm_kernel, nk=nk, act=act, has_bias=bias is not None,
                          has_res=residual is not None, cast_w=cast_w, has_side=side_cast is not None),
        grid=grid,
        in_specs=in_specs,
        out_specs=out_specs,
        out_shape=out_shape,
        scratch_shapes=scratch,
        compiler_params=_cparams(("arbitrary" if stream_w else "parallel",) * 2 + ("arbitrary",)),
        name=name,
    )(*args)


def _glu_mm_kernel(x_ref, wa_ref, wb_ref, ba_ref, bb_ref, o_ref, wab_ref, wbb_ref):
    @pl.when(pl.program_id(1) == 0)
    def _():
        wab_ref[...] = wa_ref[...].astype(BF16)
        wbb_ref[...] = wb_ref[...].astype(BF16)

    x = x_ref[...]
    a = jnp.dot(x, wab_ref[...], preferred_element_type=F32) + ba_ref[...]
    b = jnp.dot(x, wbb_ref[...], preferred_element_type=F32) + bb_ref[...]
    o_ref[...] = (a * jax.nn.sigmoid(b)).astype(o_ref.dtype)


def _glu_matmul(x, w, w_index, bias):
    m, kd = x.shape
    n = w.shape[-1] // 2
    tm, tn = _tile(m, 512), _tile(n, 256)
    nj = n // tn
    b2 = bias.reshape(1, 2 * n).astype(F32)
    return pl.pallas_call(
        _glu_mm_kernel,
        grid=(nj, m // tm),
        in_specs=[pl.BlockSpec((tm, kd), lambda j, i: (i, 0)),
                  pl.BlockSpec((None, kd, tn), lambda j, i: (w_index, 0, j)),
                  pl.BlockSpec((None, kd, tn), lambda j, i: (w_index, 0, j + nj)),
                  pl.BlockSpec((1, tn), lambda j, i: (0, j)),
                  pl.BlockSpec((1, tn), lambda j, i: (0, j + nj))],
        out_specs=pl.BlockSpec((tm, tn), lambda j, i: (i, j)),
        out_shape=jax.ShapeDtypeStruct((m, n), F32),
        scratch_shapes=[pltpu.VMEM((kd, tn), BF16), pltpu.VMEM((kd, tn), BF16)],
        compiler_params=_cparams(("arbitrary", "arbitrary")),
        name="conv_pw1_glu",
    )(x, w, w, b2, b2)


def _bdot(a, b):
    return jnp.dot(a.astype(BF16), b.astype(BF16), preferred_element_type=F32)


def _bdot_nt(a, b):
    return lax.dot_general(a.astype(BF16), b.astype(BF16), (((1,), (1,)), ((), ())),
                           preferred_element_type=F32)


def _split(a):
    hi = a.astype(BF16)
    lo = (a - hi.astype(F32)).astype(BF16)
    return hi, lo


def _blockdiag(y):
    left = lax.broadcasted_iota(jnp.int32, y.shape, 1) < y.shape[0]
    return jnp.concatenate([jnp.where(left, y, 0.0), jnp.where(left, 0.0, y)], axis=0)


def _hdot(x, y):
    return _bdot(x, _blockdiag(y))


def _unit_lower_inverses(n_list, eye2, xr2):
    n8 = [jnp.where(xr2 < 8, n, 0.0) for n in n_list]
    n8_2 = [_hdot(x, x) for x in n8]
    n8_4 = [_hdot(x, x) for x in n8_2]
    n8_3 = [_hdot(x, y) for x, y in zip(n8, n8_2)]
    t = [eye2 + a + b + c for a, b, c in zip(n8, n8_2, n8_3)]
    t = [x + _hdot(x, y) for x, y in zip(t, n8_4)]
    k = 8
    while k < CHUNK:
        level = (xr2 >= k) & (xr2 < 2 * k)
        u = [_hdot(jnp.where(level, n, 0.0), x) for n, x in zip(n_list, t)]
        t = [x + _hdot(x, y) for x, y in zip(t, u)]
        k *= 2
    return t


def _wkv_kernel(*refs, pairs, has_vres):
    if has_vres:
        (r_ref, k_ref, v_ref, wl_ref, al_ref, g_ref, vf_ref, vl_ref,
         w0_ref, a0_ref, kk_ref, ka_ref, rk_ref, lg_ref, lb_ref, v0_ref, o_ref, s_ref) = refs
    else:
        (r_ref, k_ref, v_ref, wl_ref, al_ref, g_ref,
         w0_ref, a0_ref, kk_ref, ka_ref, rk_ref, lg_ref, lb_ref, o_ref, s_ref) = refs
        vf_ref = vl_ref = v0_ref = None

    @pl.when(pl.program_id(1) == 0)
    def _():
        s_ref[...] = jnp.zeros_like(s_ref)

    L = CHUNK
    row = lax.broadcasted_iota(jnp.int32, (L, L), 0)
    col = lax.broadcasted_iota(jnp.int32, (L, L), 1)
    lower = row >= col
    same_head = (row >= HEAD) == (col >= HEAD)
    eye = (row == col).astype(F32)
    ltri = lower.astype(BF16)
    seg = same_head.astype(BF16)
    row2 = lax.broadcasted_iota(jnp.int32, (L, 2 * L), 0)
    col2 = lax.broadcasted_iota(jnp.int32, (L, 2 * L), 1) % L
    lower2 = row2 >= col2
    strict2 = row2 > col2
    xr2 = row2 ^ col2
    eye2 = (row2 == col2).astype(F32)

    def both(a):
        h0 = lax.broadcasted_iota(jnp.int32, a.shape, 1) % LANES < HEAD
        return jnp.concatenate([jnp.where(h0, a, 0.0), jnp.where(h0, 0.0, a)], axis=0)

    slabs = [slice(p * LANES, (p + 1) * LANES) for p in range(pairs)]

    def segsum(x):
        return _bdot(x, seg)

    def prepare(sl):
        r = r_ref[:, sl]
        k = k_ref[:, sl]
        v = v_ref[:, sl]
        z = w0_ref[:, sl] + wl_ref[:, sl]
        w = jnp.minimum(z, 0.0) - jnp.log(1.0 + jnp.exp(-jnp.abs(z))) - 0.5
        lw = -jnp.exp(w)
        a = jax.nn.sigmoid(a0_ref[:, sl] + al_ref[:, sl])
        if has_vres:
            v = v + (vf_ref[:, sl] - v) * jax.nn.sigmoid(v0_ref[:, sl] + vl_ref[:, sl])
        kk = k * kk_ref[:, sl]
        kk = kk / jnp.maximum(jnp.sqrt(segsum(kk * kk)), 1e-12)
        k = k * (1.0 + (a - 1.0) * ka_ref[:, sl])
        avec = -kk
        bvec = kk * a
        lw_hi, lw_lo = _split(lw)
        cum = (jnp.dot(ltri, lw_hi, preferred_element_type=F32)
               + jnp.dot(ltri, lw_lo, preferred_element_type=F32))
        cmid = cum[L // 2 - 1:L // 2, :]
        clast = cum[L - 1:L, :]
        e_neg = jnp.exp(cmid - cum)
        e_last = jnp.exp(clast - cum)
        return dict(
            v=v, rkb=r * k * rk_ref[:, sl],
            p_last=jnp.exp(clast), p_mid=jnp.exp(cmid),
            r_h=r * jnp.exp(cum - cmid),
            a_h=avec * jnp.exp(cum - lw - cmid),
            b_t=bvec * e_neg, k_t=k * e_neg, b_d=bvec * e_last, k_d=k * e_last)

    states = [s_ref[p] for p in range(pairs)]

    def sbs(att, r0, c0):
        return jnp.concatenate([att[r0:r0 + L, c0:c0 + L], att[r0 + 2 * L:r0 + 3 * L, c0:c0 + L]], axis=1)

    def interactions(q):
        return _bdot_nt(both(jnp.concatenate([q["a_h"], q["r_h"]], axis=0)),
                        jnp.concatenate([q["b_t"], q["k_t"]], axis=0))

    half = max(pairs // 2, 1)
    pre = [prepare(sl) for sl in slabs[:half]]
    att = [interactions(q) for q in pre]
    pre += [prepare(sl) for sl in slabs[half:]]
    a_ab = [jnp.where(strict2, sbs(x, 0, 0), 0.0) for x in att]
    t_inv = _unit_lower_inverses(a_ab, eye2, xr2)
    att += [interactions(q) for q in pre[half:]]
    a_ab += [jnp.where(strict2, sbs(x, 0, 0), 0.0) for x in att[half:]]
    t_inv += _unit_lower_inverses(a_ab[half:], eye2, xr2)
    a_ak = [jnp.where(strict2, sbs(x, 0, L), 0.0) for x in att]
    a_rb = [jnp.where(lower2, sbs(x, L, 0), 0.0) for x in att]
    a_rk = [jnp.where(lower2, sbs(x, L, L), 0.0) for x in att]

    v2 = [both(q["v"]) for q in pre]
    akv = [_bdot(x, y) for x, y in zip(a_ak, v2)]
    wu = [_bdot(t, both(jnp.concatenate([q["a_h"], u], axis=1)))
          for t, u, q in zip(t_inv, akv, pre)]
    qy = [_bdot(x, both(y)) for x, y in zip(a_rb, wu)]
    yk = [_bdot(x, y) for x, y in zip(a_rk, v2)]

    q_t = [(q["r_h"] + x[:, :LANES]) * q["p_mid"] for q, x in zip(pre, qy)]
    y_in = [x[:, LANES:] + z for x, z in zip(qy, yk)]
    bdt = [q["b_d"].T for q in pre]
    kdt = [q["k_d"].T for q in pre]
    m2 = [eye * q["p_last"] + jnp.where(same_head, _bdot(b, x[:, :LANES] * q["p_mid"]), 0.0)
          for q, b, x in zip(pre, bdt, wu)]
    g2 = [jnp.where(same_head, _bdot(jnp.concatenate([b, kd], axis=1),
                                     jnp.concatenate([x[:, LANES:], q["v"]], axis=0)), 0.0)
          for q, b, kd, x in zip(pre, bdt, kdt, wu)]
    y = [_bdot(x, s) + z for x, s, z in zip(q_t, states, y_in)]
    new_states = [_bdot(m, s) + g for m, s, g in zip(m2, states, g2)]

    d = [x - segsum(x) * (1.0 / HEAD) for x in y]
    var = [segsum(x * x) * (1.0 / HEAD) for x in d]
    bonus = [segsum(q["rkb"]) * q["v"] for q in pre]
    for p, sl in enumerate(slabs):
        s_ref[p] = new_states[p]
        yn = d[p] * lax.rsqrt(var[p] + GN_EPS) * lg_ref[:, sl] + lb_ref[:, sl]
        o_ref[:, sl] = ((yn + bonus[p]) * g_ref[:, sl]).astype(o_ref.dtype)


def _wkv7(r, k, v, wl, al, g, w0, a0, k_k, k_a, r_k, lnx_g, lnx_b, vres=None):
    t, d = r.shape
    pairs = min(WKV_PAIRS, d // LANES)
    pw = pairs * LANES
    assert t % CHUNK == 0 and d % pw == 0
    seq = pl.BlockSpec((CHUNK, pw), lambda j, c: (c, j))
    par = pl.BlockSpec((1, pw), lambda j, c: (0, j))
    row = lambda a: a.reshape(1, d).astype(F32)
    seq_args = [r, k, v, wl, al, g]
    par_args = [row(w0), row(a0), row(k_k), row(k_a), row(r_k), row(lnx_g), row(lnx_b)]
    if vres is not None:
        v_first, vl, v0 = vres
        seq_args += [v_first, vl]
        par_args += [row(v0)]
    return pl.pallas_call(
        functools.partial(_wkv_kernel, pairs=pairs, has_vres=vres is not None),
        grid=(d // pw, t // CHUNK),
        in_specs=[seq] * len(seq_args) + [par] * len(par_args),
        out_specs=seq,
        out_shape=jax.ShapeDtypeStruct((t, d), BF16),
        scratch_shapes=[pltpu.VMEM((pairs, LANES, LANES), F32)],
        compiler_params=_cparams(("parallel", "arbitrary")),
        name="wkv7",
    )(*seq_args, *par_args)


def _swa_kernel(sink_ref, q_ref, kc_ref, kp_ref, vc_ref, vp_ref, o_ref, *, group):
    n = pl.program_id(0)
    W = WINDOW
    qi = lax.broadcasted_iota(jnp.int32, (2 * W, 2 * W), 0) % W
    kj = lax.broadcasted_iota(jnp.int32, (2 * W, 2 * W), 1)
    valid = (kj > qi) & (kj <= qi + W) & ((kj >= W) | (n > 0))
    top = lax.broadcasted_iota(jnp.int32, (2 * W, 1), 0) < W
    lane = lax.broadcasted_iota(jnp.int32, (W, LANES), 1)
    head0 = lane < HEAD
    zero = jnp.zeros((), BF16)
    for hk in range(SWA_KV_HEADS):
        ks = slice(hk * LANES, (hk + 1) * LANES)
        k2 = jnp.concatenate([kp_ref[:, ks], kc_ref[:, ks]], axis=0)
        v2 = jnp.concatenate([vp_ref[:, ks], vc_ref[:, ks]], axis=0)
        for i in range(group // 2):
            slab = hk * (group // 2) + i
            qs = q_ref[:, slab * LANES:(slab + 1) * LANES]
            q2 = jnp.concatenate([jnp.where(head0, qs, zero), jnp.where(head0, zero, qs)], axis=0)
            s = lax.dot_general(q2, k2, (((1,), (1,)), ((), ())), preferred_element_type=F32)
            s = jnp.where(valid, s, NEG_INF)
            sink = jnp.where(top, sink_ref[2 * slab], sink_ref[2 * slab + 1])
            m = jnp.maximum(jnp.max(s, axis=-1, keepdims=True), sink)
            e = jnp.exp(s - m)
            den = jnp.sum(e, axis=-1, keepdims=True) + jnp.exp(sink - m)
            pr = (e / den).astype(BF16)
            o = jnp.dot(pr, v2, preferred_element_type=F32)
            o_ref[:, slab * LANES:(slab + 1) * LANES] = jnp.where(head0, o[:W], o[W:]).astype(o_ref.dtype)


def _swa_attention(qkv, sinks, d):
    t = qkv.shape[0]
    kvw = SWA_KV_HEADS * LANES
    qb = d // kvw
    group = d // HEAD // SWA_KV_HEADS
    assert group % 2 == 0 and d % kvw == 0 and t % WINDOW == 0
    prev = lambda n: jnp.maximum(n - 1, 0)
    return pl.pallas_call(
        functools.partial(_swa_kernel, group=group),
        grid=(t // WINDOW,),
        in_specs=[pl.BlockSpec(memory_space=pltpu.SMEM),
                  pl.BlockSpec((WINDOW, d), lambda n: (n, 0)),
                  pl.BlockSpec((WINDOW, kvw), lambda n: (n, qb)),
                  pl.BlockSpec((WINDOW, kvw), lambda n: (prev(n), qb)),
                  pl.BlockSpec((WINDOW, kvw), lambda n: (n, qb + 1)),
                  pl.BlockSpec((WINDOW, kvw), lambda n: (prev(n), qb + 1))],
        out_specs=pl.BlockSpec((WINDOW, d), lambda n: (n, 0)),
        out_shape=jax.ShapeDtypeStruct((t, d), BF16),
        compiler_params=_cparams(("arbitrary",)),
        name="swa_attention",
    )(sinks.astype(F32), qkv, qkv, qkv, qkv, qkv)


def _conv_kernel(u_ref, halo_ref, w_ref, b_ref, g_ref, beta_ref, o_ref, buf_ref, acc_ref):
    tm, d = u_ref.shape
    first = pl.program_id(0) == 0
    buf_ref[0:CONV_HALO, :] = jnp.where(first, 0.0, halo_ref[...])
    buf_ref[CONV_HALO:, :] = u_ref[...]
    rb = min(tm, 64)
    base = CONV_HALO - (CONV_WIDTH - 1)

    def col_body(c, carry):
        cs = pl.ds(pl.multiple_of(c * LANES, LANES), LANES)
        for r0 in range(0, tm, rb):
            acc = None
            for s in range(SUBLANES):
                rows = rb if s == 0 else rb + SUBLANES
                part = None
                for o in range(base, base + CONV_WIDTH):
                    if o % SUBLANES != s:
                        continue
                    term = buf_ref[pl.ds(r0 + o - s, rows), cs] * w_ref[pl.ds(o - base, 1), cs]
                    part = term if part is None else part + term
                if s:
                    part = pltpu.roll(part, shift=rows - s, axis=0)[:rb]
                acc = part if acc is None else acc + part
            acc_ref[pl.ds(r0, rb), cs] = acc
        return carry

    lax.fori_loop(0, d // LANES, col_body, 0)
    y = acc_ref[...] + b_ref[...]
    mu = jnp.mean(y, axis=-1, keepdims=True)
    yc = y - mu
    var = jnp.mean(yc * yc, axis=-1, keepdims=True)
    yn = yc * lax.rsqrt(var + LN_EPS) * g_ref[...] + beta_ref[...]
    o_ref[...] = (yn * jax.nn.sigmoid(yn)).astype(o_ref.dtype)


def _conv_ln_silu(u, w_dw, b_dw, ln_g, ln_b):
    t, d = u.shape
    tm = _tile(t, 256)
    assert tm % CONV_HALO == 0
    ratio = tm // CONV_HALO
    row = lambda a: a.reshape(1, d).astype(F32)
    vec = pl.BlockSpec((1, d), lambda i: (0, 0))
    return pl.pallas_call(
        _conv_kernel,
        grid=(t // tm,),
        in_specs=[pl.BlockSpec((tm, d), lambda i: (i, 0)),
                  pl.BlockSpec((CONV_HALO, d), lambda i: (jnp.maximum(i * ratio - 1, 0), 0)),
                  pl.BlockSpec((CONV_WIDTH, d), lambda i: (0, 0)),
                  vec, vec, vec],
        out_specs=pl.BlockSpec((tm, d), lambda i: (i, 0)),
        out_shape=jax.ShapeDtypeStruct((t, d), BF16),
        scratch_shapes=[pltpu.VMEM((tm + CONV_HALO, d), F32), pltpu.VMEM((tm, d), F32)],
        compiler_params=_cparams(("arbitrary",)),
        name="conv_dw_ln_silu",
    )(u, u, w_dw.astype(F32), row(b_dw), row(ln_g), row(ln_b))


def _pad_to(a, axis, mult):
    pad = (-a.shape[axis]) % mult
    if pad == 0:
        return a
    widths = [(0, 0)] * a.ndim
    widths[axis] = (0, pad)
    return jnp.pad(a, widths)


def _rwkv_layer(x, norm_g, v_first, vres, mix, w_rkv, layer, w0, w1, w2, a0, a1, a2, g1, g2,
                k_k, k_a, r_k, lnx_g, lnx_b, w_o):
    d = x.shape[1]
    branches = [(5, g1, g2, "sigmoid"), (3, w1, w2, "tanh"), (4, a1, a2, None)]
    if vres is not None:
        branches.append((2, vres[1], vres[2], None))
    downs = [_pad_to(b[1], 1, LANES) for b in branches]
    widths = [w.shape[1] for w in downs]
    offs = [sum(widths[:i]) for i in range(len(widths))]
    assert all(o % w == 0 for o, w in zip(offs, widths))
    w_down = jnp.concatenate(downs, axis=1).astype(BF16)
    loras = tuple((b[0], o, w, b[3]) for b, o, w in zip(branches, offs, widths))
    xs, mids = _rwkv_mix(x, norm_g, mix, w_down, loras)
    r, k, v = (_matmul(xs, w_rkv, x_index=c, w_index=3 * layer + c, name="rwkv_" + "rkv"[c])
               for c in range(3))
    ups = [_matmul(mids, _pad_to(b[2], 0, LANES).astype(BF16), x_col=o // w, kd=w,
                   out_dtype=F32 if b[0] == 3 else BF16, name="rwkv_lora_up")
           for b, o, w in zip(branches, offs, widths)]
    g, wl, al = ups[:3]
    if vres is None:
        v_first = v
        vres_args = None
    else:
        vres_args = (v_first, ups[3], vres[0])
    z = _wkv7(r, k, v, wl, al, g, w0, a0, k_k, k_a, r_k.reshape(d), lnx_g, lnx_b, vres=vres_args)
    return _matmul(z, w_o, w_index=layer, residual=x, name="rwkv_out"), v_first


def _swa_layer(x, norm_g, w_qkv, b_qkv, sinks, w_o, layer, b_o):
    d = x.shape[1]
    kvw = SWA_KV_HEADS * HEAD
    scale = HEAD ** -0.5

    def dup(a):
        a = a.reshape(a.shape[:-1] + (SWA_KV_HEADS, 1, HEAD))
        a = jnp.broadcast_to(a, a.shape[:-2] + (2, HEAD))
        return a.reshape(a.shape[:-3] + (2 * kvw,))

    w_ext = jnp.concatenate([w_qkv[:, :d] * scale, dup(w_qkv[:, d:d + kvw]), dup(w_qkv[:, d + kvw:])], axis=1)
    b_ext = jnp.concatenate([b_qkv[:d] * scale, dup(b_qkv[d:d + kvw]), dup(b_qkv[d + kvw:])], axis=0)
    h = _rmsnorm(x, norm_g, BF16)
    qkv = _matmul(h, w_ext.astype(BF16), bias=b_ext, out_dtype=BF16, name="swa_qkv")
    o = _swa_attention(qkv, sinks, d)
    return _matmul(o, w_o, w_index=layer, bias=b_o, residual=x, name="swa_out")


def _conv_layer(x, norm_g, w_pw1, layer, b_pw1, w_dw, b_dw, ln_g, ln_b, w_pw2, b_pw2):
    h = _rmsnorm(x, norm_g, BF16)
    u = _glu_matmul(h, w_pw1, layer, b_pw1)
    u = _conv_ln_silu(u, w_dw, b_dw, ln_g, ln_b)
    return _matmul(u, w_pw2, w_index=layer, bias=b_pw2, residual=x, name="conv_pw2")


def _mlp(x, norm_g, w_in, w_out, layer):
    h = _rmsnorm(x, norm_g, BF16)
    a, w_out_bf16 = _matmul(h, w_in, w_index=layer, act="relu2", out_dtype=BF16,
                            side_cast=(w_out, layer), name="mlp_in")
    return _matmul(a, w_out_bf16, residual=x, tm=512, tn=1024, tk=4096, name="mlp_out")


def kernel(x, norm_mix_g, norm_mlp_g, norm_f_g, rwkv_mix, rwkv_w_rkv, rwkv_w0, rwkv_w1, rwkv_w2, rwkv_a0, rwkv_a1, rwkv_a2, rwkv_v0, rwkv_v1, rwkv_v2, rwkv_g1, rwkv_g2, rwkv_k_k, rwkv_k_a, rwkv_r_k, rwkv_lnx_g, rwkv_lnx_b, rwkv_w_o, swa_w_qkv, swa_b_qkv, swa_sinks, swa_w_o, swa_b_o, conv_w_pw1, conv_b_pw1, conv_w_dw, conv_b_dw, conv_ln_g, conv_ln_b, conv_w_pw2, conv_b_pw2, mlp_w_in, mlp_w_out):
    b, t, d = x.shape
    assert b == 1, "token shift and the WKV scan are written for a single sequence"
    depth = norm_mix_g.shape[0]
    xs = x.reshape(t, d)
    w_rkv = rwkv_w_rkv.reshape((-1,) + rwkv_w_rkv.shape[2:])
    ia = ib = ic = 0
    v_first = None
    for i in range(depth):
        kind = i % 3
        if kind == 0:
            vres = None if ia == 0 else (rwkv_v0[ia - 1], rwkv_v1[ia - 1], rwkv_v2[ia - 1])
            xs, v_first = _rwkv_layer(xs, norm_mix_g[i], v_first, vres, rwkv_mix[ia], w_rkv, ia,
                                      rwkv_w0[ia], rwkv_w1[ia], rwkv_w2[ia], rwkv_a0[ia], rwkv_a1[ia],
                                      rwkv_a2[ia], rwkv_g1[ia], rwkv_g2[ia], rwkv_k_k[ia], rwkv_k_a[ia],
                                      rwkv_r_k[ia], rwkv_lnx_g[ia], rwkv_lnx_b[ia], rwkv_w_o)
            ia += 1
        elif kind == 1:
            xs = _swa_layer(xs, norm_mix_g[i], swa_w_qkv[ib], swa_b_qkv[ib], swa_sinks[ib],
                            swa_w_o, ib, swa_b_o[ib])
            ib += 1
        else:
            xs = _conv_layer(xs, norm_mix_g[i], conv_w_pw1, ic, conv_b_pw1[ic], conv_w_dw[ic],
                             conv_b_dw[ic], conv_ln_g[ic], conv_ln_b[ic], conv_w_pw2, conv_b_pw2[ic])
            ic += 1
        xs = _mlp(xs, norm_mlp_g[i], mlp_w_in, mlp_w_out, i)
    return _rmsnorm(xs, norm_f_g, F32).reshape(b, t, d)
```

```python
import functools

import jax
import jax.numpy as jnp
from jax import lax
from jax.experimental import pallas as pl
from jax.experimental.pallas import tpu as pltpu

F32 = jnp.float32
BF16 = jnp.bfloat16

LANES = 128
SUBLANES = 8
HEAD = 64
CHUNK = 128
WKV_PAIRS = 16
WINDOW = 128
SWA_KV_HEADS = 8
CONV_WIDTH = 31
CONV_HALO = 32
RMS_EPS = 1e-5
LN_EPS = 1e-5
GN_EPS = 64e-5
NEG_INF = -1e30
VMEM_LIMIT = 56 * 1024 * 1024


def _cparams(sem):
    return pltpu.CompilerParams(dimension_semantics=sem, vmem_limit_bytes=VMEM_LIMIT)


def _tile(n, pref):
    if n <= pref:
        return n
    t = pref
    while n % t:
        t //= 2
    return t


def _rmsnorm_kernel(x_ref, g_ref, o_ref):
    x = x_ref[...]
    y = x * lax.rsqrt(jnp.mean(x * x, axis=-1, keepdims=True) + RMS_EPS)
    o_ref[...] = (y * g_ref[...]).astype(o_ref.dtype)


def _rmsnorm(x, g, out_dtype):
    m, d = x.shape
    tm = _tile(m, 256)
    return pl.pallas_call(
        _rmsnorm_kernel,
        grid=(m // tm,),
        in_specs=[pl.BlockSpec((tm, d), lambda i: (i, 0)),
                  pl.BlockSpec((1, d), lambda i: (0, 0))],
        out_specs=pl.BlockSpec((tm, d), lambda i: (i, 0)),
        out_shape=jax.ShapeDtypeStruct((m, d), out_dtype),
        compiler_params=_cparams(("arbitrary",)),
        name="rmsnorm",
    )(x, g.reshape(1, d))


def _rwkv_mix_kernel(x_ref, g_ref, mix_ref, wd_ref, o_ref, mid_ref, carry_ref, *, loras):
    @pl.when(pl.program_id(0) == 0)
    def _():
        carry_ref[...] = jnp.zeros_like(carry_ref)

    x = x_ref[...]
    h = x * lax.rsqrt(jnp.mean(x * x, axis=-1, keepdims=True) + RMS_EPS) * g_ref[...]
    tm = h.shape[0]
    prev = pltpu.roll(h, shift=1, axis=0)
    row = lax.broadcasted_iota(jnp.int32, h.shape, 0)
    prev = jnp.where(row == 0, carry_ref[...], prev)
    carry_ref[...] = h[tm - 1:tm, :]
    xx = prev - h
    streams = {}

    def stream(c):
        if c not in streams:
            streams[c] = (h + xx * mix_ref[c:c + 1, :]).astype(BF16)
        return streams[c]

    for c in range(3):
        o_ref[c] = stream(c)
    for c, off, width, act in loras:
        mid = jnp.dot(stream(c), wd_ref[:, off:off + width], preferred_element_type=F32)
        mid_ref[:, off:off + width] = _activate(mid, act).astype(mid_ref.dtype)


def _rwkv_mix(x, g, mix, w_down, loras):
    m, d = x.shape
    nl = w_down.shape[1]
    tm = _tile(m, 256)
    return pl.pallas_call(
        functools.partial(_rwkv_mix_kernel, loras=loras),
        grid=(m // tm,),
        in_specs=[pl.BlockSpec((tm, d), lambda i: (i, 0)),
                  pl.BlockSpec((1, d), lambda i: (0, 0)),
                  pl.BlockSpec((6, d), lambda i: (0, 0)),
                  pl.BlockSpec((d, nl), lambda i: (0, 0))],
        out_specs=[pl.BlockSpec((3, tm, d), lambda i: (0, i, 0)),
                   pl.BlockSpec((tm, nl), lambda i: (i, 0))],
        out_shape=[jax.ShapeDtypeStruct((3, m, d), BF16), jax.ShapeDtypeStruct((m, nl), BF16)],
        scratch_shapes=[pltpu.VMEM((1, d), F32)],
        compiler_params=_cparams(("arbitrary",)),
        name="rwkv_mix",
    )(x, g.reshape(1, d), mix, w_down)


def _activate(acc, act):
    if act is None:
        return acc
    if act == "tanh":
        return jnp.tanh(acc)
    if act == "sigmoid":
        return jax.nn.sigmoid(acc)
    if act == "relu2":
        return jnp.square(jnp.maximum(acc, 0.0))
    raise ValueError(act)


def _mm_kernel(*refs, nk, act, has_bias, has_res, cast_w, has_side):
    x_ref, w_ref = refs[0], refs[1]
    pos = 2
    b_ref = r_ref = None
    if has_bias:
        b_ref = refs[pos]
        pos += 1
    if has_res:
        r_ref = refs[pos]
        pos += 1
    if has_side:
        side_in_ref, side_out_ref = refs[pos], refs[pos + 2]

        @pl.when(pl.program_id(2) == 0)
        def _():
            side_out_ref[...] = side_in_ref[...].astype(side_out_ref.dtype)
        o_ref = refs[pos + 1]
        pos += 2
    else:
        o_ref = refs[pos]
    scratch = list(refs[pos + 1:])

    def epilogue(acc):
        if has_bias:
            acc = acc + b_ref[...]
        acc = _activate(acc, act)
        if has_res:
            acc = acc + r_ref[...]
        o_ref[...] = acc.astype(o_ref.dtype)

    if cast_w:
        wb_ref = scratch.pop(0)

        @pl.when(pl.program_id(1) == 0)
        def _():
            wb_ref[...] = w_ref[...].astype(BF16)

        w = wb_ref[...]
    else:
        w = w_ref[...]
    part = jnp.dot(x_ref[...], w, preferred_element_type=F32)
    if nk == 1:
        epilogue(part)
        return
    acc_ref = scratch.pop(0)
    k = pl.program_id(2)

    @pl.when(k == 0)
    def _():
        acc_ref[...] = part

    @pl.when(k > 0)
    def _():
        acc_ref[...] += part

    @pl.when(k == nk - 1)
    def _():
        epilogue(acc_ref[...])


def _matmul(x, w, *, name, x_index=None, x_col=0, kd=None, w_index=None, bias=None, act=None,
            residual=None, out_dtype=F32, tm=None, tn=None, tk=4096, side_cast=None):
    m = x.shape[-2]
    kd = x.shape[-1] if kd is None else kd
    n = w.shape[-1]
    assert w.shape[-2] == kd
    cast_w = w.dtype != BF16
    if tn is None:
        tn = 512 if cast_w else 1024
    if tm is None:
        tm = 1024 if cast_w else 512
    tm, tn, tk = _tile(m, tm), _tile(n, tn), _tile(kd, tk)
    nk = kd // tk
    assert not cast_w or nk == 1, "the cast-once path needs the weight block fixed across row tiles"
    assert x_col == 0 or nk == 1
    if w.ndim == 3:
        w_spec = pl.BlockSpec((None, tk, tn), lambda j, i, k: (w_index, k, j))
    else:
        w_spec = pl.BlockSpec((tk, tn), lambda j, i, k: (k, j))
    if x.ndim == 3:
        x_spec = pl.BlockSpec((None, tm, tk), lambda j, i, k: (x_index, i, k + x_col))
    else:
        x_spec = pl.BlockSpec((tm, tk), lambda j, i, k: (i, k + x_col))
    in_specs = [x_spec, w_spec]
    args = [x, w]
    if bias is not None:
        in_specs.append(pl.BlockSpec((1, tn), lambda j, i, k: (0, j)))
        args.append(bias.reshape(1, n).astype(F32))
    if residual is not None:
        in_specs.append(pl.BlockSpec((tm, tn), lambda j, i, k: (i, j)))
        args.append(residual)
    grid = (n // tn, m // tm, nk)
    out_specs = pl.BlockSpec((tm, tn), lambda j, i, k: (i, j))
    out_shape = jax.ShapeDtypeStruct((m, n), out_dtype)
    if side_cast is not None:
        stack, s_index = side_cast
        rows, cols = stack.shape[-2:]
        steps = grid[0] * grid[1]
        assert rows % (steps * 16) == 0
        slab = rows // steps
        in_specs.append(pl.BlockSpec((None, slab, cols), lambda j, i, k: (s_index, j * grid[1] + i, 0)))
        args.append(stack)
        out_specs = [out_specs, pl.BlockSpec((slab, cols), lambda j, i, k: (j * grid[1] + i, 0))]
        out_shape = [out_shape, jax.ShapeDtypeStruct((rows, cols), BF16)]
    scratch = []
    if cast_w:
        scratch.append(pltpu.VMEM((tk, tn), BF16))
    if nk > 1:
        scratch.append(pltpu.VMEM((tm, tn), F32))
    return pl.pallas_call(
        functools.partial(_mm_kernel, nk=nk, act=act, has_bias=bias is not None,
                          has_res=residual is not None, cast_w=cast_w, has_side=side_cast is not None),
        grid=grid,
        in_specs=in_specs,
        out_specs=out_specs,
        out_shape=out_shape,
        scratch_shapes=scratch,
        compiler_params=_cparams(("arbitrary" if cast_w else "parallel",) * 2 + ("arbitrary",)),
        name=name,
    )(*args)


def _glu_mm_kernel(x_ref, wa_ref, wb_ref, ba_ref, bb_ref, o_ref, wab_ref, wbb_ref):
    @pl.when(pl.program_id(1) == 0)
    def _():
        wab_ref[...] = wa_ref[...].astype(BF16)
        wbb_ref[...] = wb_ref[...].astype(BF16)

    x = x_ref[...]
    a = jnp.dot(x, wab_ref[...], preferred_element_type=F32) + ba_ref[...]
    b = jnp.dot(x, wbb_ref[...], preferred_element_type=F32) + bb_ref[...]
    o_ref[...] = (a * jax.nn.sigmoid(b)).astype(o_ref.dtype)


def _glu_matmul(x, w, w_index, bias):
    m, kd = x.shape
    n = w.shape[-1] // 2
    tm, tn = _tile(m, 512), _tile(n, 256)
    nj = n // tn
    b2 = bias.reshape(1, 2 * n).astype(F32)
    return pl.pallas_call(
        _glu_mm_kernel,
        grid=(nj, m // tm),
        in_specs=[pl.BlockSpec((tm, kd), lambda j, i: (i, 0)),
                  pl.BlockSpec((None, kd, tn), lambda j, i: (w_index, 0, j)),
                  pl.BlockSpec((None, kd, tn), lambda j, i: (w_index, 0, j + nj)),
                  pl.BlockSpec((1, tn), lambda j, i: (0, j)),
                  pl.BlockSpec((1, tn), lambda j, i: (0, j + nj))],
        out_specs=pl.BlockSpec((tm, tn), lambda j, i: (i, j)),
        out_shape=jax.ShapeDtypeStruct((m, n), F32),
        scratch_shapes=[pltpu.VMEM((kd, tn), BF16), pltpu.VMEM((kd, tn), BF16)],
        compiler_params=_cparams(("arbitrary", "arbitrary")),
        name="conv_pw1_glu",
    )(x, w, w, b2, b2)


def _bdot(a, b):
    return jnp.dot(a.astype(BF16), b.astype(BF16), preferred_element_type=F32)


def _bdot_nt(a, b):
    return lax.dot_general(a.astype(BF16), b.astype(BF16), (((1,), (1,)), ((), ())),
                           preferred_element_type=F32)


def _split(a):
    hi = a.astype(BF16)
    lo = (a - hi.astype(F32)).astype(BF16)
    return hi, lo


def _blockdiag(y):
    left = lax.broadcasted_iota(jnp.int32, y.shape, 1) < y.shape[0]
    return jnp.concatenate([jnp.where(left, y, 0.0), jnp.where(left, 0.0, y)], axis=0)


def _hdot(x, y):
    return _bdot(x, _blockdiag(y))


def _unit_lower_inverses(n_list, eye2, xr2):
    n8 = [jnp.where(xr2 < 8, n, 0.0) for n in n_list]
    n8_2 = [_hdot(x, x) for x in n8]
    n8_4 = [_hdot(x, x) for x in n8_2]
    n8_3 = [_hdot(x, y) for x, y in zip(n8, n8_2)]
    t = [eye2 + a + b + c for a, b, c in zip(n8, n8_2, n8_3)]
    t = [x + _hdot(x, y) for x, y in zip(t, n8_4)]
    k = 8
    while k < CHUNK:
        level = (xr2 >= k) & (xr2 < 2 * k)
        u = [_hdot(jnp.where(level, n, 0.0), x) for n, x in zip(n_list, t)]
        t = [x + _hdot(x, y) for x, y in zip(t, u)]
        k *= 2
    return t


def _wkv_kernel(*refs, pairs, has_vres):
    if has_vres:
        (r_ref, k_ref, v_ref, wl_ref, al_ref, g_ref, vf_ref, vl_ref,
         w0_ref, a0_ref, kk_ref, ka_ref, rk_ref, lg_ref, lb_ref, v0_ref, o_ref, s_ref) = refs
    else:
        (r_ref, k_ref, v_ref, wl_ref, al_ref, g_ref,
         w0_ref, a0_ref, kk_ref, ka_ref, rk_ref, lg_ref, lb_ref, o_ref, s_ref) = refs
        vf_ref = vl_ref = v0_ref = None

    @pl.when(pl.program_id(1) == 0)
    def _():
        s_ref[...] = jnp.zeros_like(s_ref)

    L = CHUNK
    row = lax.broadcasted_iota(jnp.int32, (L, L), 0)
    col = lax.broadcasted_iota(jnp.int32, (L, L), 1)
    lower = row >= col
    same_head = (row >= HEAD) == (col >= HEAD)
    eye = (row == col).astype(F32)
    ltri = lower.astype(BF16)
    seg = same_head.astype(BF16)
    row2 = lax.broadcasted_iota(jnp.int32, (L, 2 * L), 0)
    col2 = lax.broadcasted_iota(jnp.int32, (L, 2 * L), 1) % L
    lower2 = row2 >= col2
    strict2 = row2 > col2
    xr2 = row2 ^ col2
    eye2 = (row2 == col2).astype(F32)

    def both(a):
        h0 = lax.broadcasted_iota(jnp.int32, a.shape, 1) % LANES < HEAD
        return jnp.concatenate([jnp.where(h0, a, 0.0), jnp.where(h0, 0.0, a)], axis=0)

    slabs = [slice(p * LANES, (p + 1) * LANES) for p in range(pairs)]

    def segsum(x):
        return _bdot(x, seg)

    def prepare(sl):
        r = r_ref[:, sl]
        k = k_ref[:, sl]
        v = v_ref[:, sl]
        z = w0_ref[:, sl] + wl_ref[:, sl]
        w = jnp.minimum(z, 0.0) - jnp.log(1.0 + jnp.exp(-jnp.abs(z))) - 0.5
        lw = -jnp.exp(w)
        a = jax.nn.sigmoid(a0_ref[:, sl] + al_ref[:, sl])
        if has_vres:
            v = v + (vf_ref[:, sl] - v) * jax.nn.sigmoid(v0_ref[:, sl] + vl_ref[:, sl])
        kk = k * kk_ref[:, sl]
        kk = kk / jnp.maximum(jnp.sqrt(segsum(kk * kk)), 1e-12)
        k = k * (1.0 + (a - 1.0) * ka_ref[:, sl])
        avec = -kk
        bvec = kk * a
        lw_hi, lw_lo = _split(lw)
        cum = (jnp.dot(ltri, lw_hi, preferred_element_type=F32)
               + jnp.dot(ltri, lw_lo, preferred_element_type=F32))
        cmid = cum[L // 2 - 1:L // 2, :]
        clast = cum[L - 1:L, :]
        e_neg = jnp.exp(cmid - cum)
        e_last = jnp.exp(clast - cum)
        return dict(
            v=v, rkb=r * k * rk_ref[:, sl],
            p_last=jnp.exp(clast), p_mid=jnp.exp(cmid),
            r_h=r * jnp.exp(cum - cmid),
            a_h=avec * jnp.exp(cum - lw - cmid),
            b_t=bvec * e_neg, k_t=k * e_neg, b_d=bvec * e_last, k_d=k * e_last)

    states = [s_ref[p] for p in range(pairs)]

    def sbs(att, r0, c0):
        return jnp.concatenate([att[r0:r0 + L, c0:c0 + L], att[r0 + 2 * L:r0 + 3 * L, c0:c0 + L]], axis=1)

    def interactions(q):
        return _bdot_nt(both(jnp.concatenate([q["a_h"], q["r_h"]], axis=0)),
                        jnp.concatenate([q["b_t"], q["k_t"]], axis=0))

    half = max(pairs // 2, 1)
    pre = [prepare(sl) for sl in slabs[:half]]
    att = [interactions(q) for q in pre]
    pre += [prepare(sl) for sl in slabs[half:]]
    a_ab = [jnp.where(strict2, sbs(x, 0, 0), 0.0) for x in att]
    t_inv = _unit_lower_inverses(a_ab, eye2, xr2)
    att += [interactions(q) for q in pre[half:]]
    a_ab += [jnp.where(strict2, sbs(x, 0, 0), 0.0) for x in att[half:]]
    t_inv += _unit_lower_inverses(a_ab[half:], eye2, xr2)
    a_ak = [jnp.where(strict2, sbs(x, 0, L), 0.0) for x in att]
    a_rb = [jnp.where(lower2, sbs(x, L, 0), 0.0) for x in att]
    a_rk = [jnp.where(lower2, sbs(x, L, L), 0.0) for x in att]

    v2 = [both(q["v"]) for q in pre]
    akv = [_bdot(x, y) for x, y in zip(a_ak, v2)]
    wu = [_bdot(t, both(jnp.concatenate([q["a_h"], u], axis=1)))
          for t, u, q in zip(t_inv, akv, pre)]
    qy = [_bdot(x, both(y)) for x, y in zip(a_rb, wu)]
    yk = [_bdot(x, y) for x, y in zip(a_rk, v2)]

    q_t = [(q["r_h"] + x[:, :LANES]) * q["p_mid"] for q, x in zip(pre, qy)]
    y_in = [x[:, LANES:] + z for x, z in zip(qy, yk)]
    bdt = [q["b_d"].T for q in pre]
    kdt = [q["k_d"].T for q in pre]
    m2 = [eye * q["p_last"] + jnp.where(same_head, _bdot(b, x[:, :LANES] * q["p_mid"]), 0.0)
          for q, b, x in zip(pre, bdt, wu)]
    g2 = [jnp.where(same_head, _bdot(jnp.concatenate([b, kd], axis=1),
                                     jnp.concatenate([x[:, LANES:], q["v"]], axis=0)), 0.0)
          for q, b, kd, x in zip(pre, bdt, kdt, wu)]
    y = [_bdot(x, s) + z for x, s, z in zip(q_t, states, y_in)]
    new_states = [_bdot(m, s) + g for m, s, g in zip(m2, states, g2)]

    d = [x - segsum(x) * (1.0 / HEAD) for x in y]
    var = [segsum(x * x) * (1.0 / HEAD) for x in d]
    bonus = [segsum(q["rkb"]) * q["v"] for q in pre]
    for p, sl in enumerate(slabs):
        s_ref[p] = new_states[p]
        yn = d[p] * lax.rsqrt(var[p] + GN_EPS) * lg_ref[:, sl] + lb_ref[:, sl]
        o_ref[:, sl] = ((yn + bonus[p]) * g_ref[:, sl]).astype(o_ref.dtype)


def _wkv7(r, k, v, wl, al, g, w0, a0, k_k, k_a, r_k, lnx_g, lnx_b, vres=None):
    t, d = r.shape
    pairs = min(WKV_PAIRS, d // LANES)
    pw = pairs * LANES
    assert t % CHUNK == 0 and d % pw == 0
    seq = pl.BlockSpec((CHUNK, pw), lambda j, c: (c, j))
    par = pl.BlockSpec((1, pw), lambda j, c: (0, j))
    row = lambda a: a.reshape(1, d).astype(F32)
    seq_args = [r, k, v, wl, al, g]
    par_args = [row(w0), row(a0), row(k_k), row(k_a), row(r_k), row(lnx_g), row(lnx_b)]
    if vres is not None:
        v_first, vl, v0 = vres
        seq_args += [v_first, vl]
        par_args += [row(v0)]
    return pl.pallas_call(
        functools.partial(_wkv_kernel, pairs=pairs, has_vres=vres is not None),
        grid=(d // pw, t // CHUNK),
        in_specs=[seq] * len(seq_args) + [par] * len(par_args),
        out_specs=seq,
        out_shape=jax.ShapeDtypeStruct((t, d), BF16),
        scratch_shapes=[pltpu.VMEM((pairs, LANES, LANES), F32)],
        compiler_params=_cparams(("parallel", "arbitrary")),
        name="wkv7",
    )(*seq_args, *par_args)


def _swa_kernel(sink_ref, q_ref, kc_ref, kp_ref, vc_ref, vp_ref, o_ref, *, group):
    n = pl.program_id(0)
    W = WINDOW
    qi = lax.broadcasted_iota(jnp.int32, (2 * W, 2 * W), 0) % W
    kj = lax.broadcasted_iota(jnp.int32, (2 * W, 2 * W), 1)
    valid = (kj > qi) & (kj <= qi + W) & ((kj >= W) | (n > 0))
    top = lax.broadcasted_iota(jnp.int32, (2 * W, 1), 0) < W
    lane = lax.broadcasted_iota(jnp.int32, (W, LANES), 1)
    head0 = lane < HEAD
    zero = jnp.zeros((), BF16)
    for hk in range(SWA_KV_HEADS):
        ks = slice(hk * LANES, (hk + 1) * LANES)
        k2 = jnp.concatenate([kp_ref[:, ks], kc_ref[:, ks]], axis=0)
        v2 = jnp.concatenate([vp_ref[:, ks], vc_ref[:, ks]], axis=0)
        for i in range(group // 2):
            slab = hk * (group // 2) + i
            qs = q_ref[:, slab * LANES:(slab + 1) * LANES]
            q2 = jnp.concatenate([jnp.where(head0, qs, zero), jnp.where(head0, zero, qs)], axis=0)
            s = lax.dot_general(q2, k2, (((1,), (1,)), ((), ())), preferred_element_type=F32)
            s = jnp.where(valid, s, NEG_INF)
            sink = jnp.where(top, sink_ref[2 * slab], sink_ref[2 * slab + 1])
            m = jnp.maximum(jnp.max(s, axis=-1, keepdims=True), sink)
            e = jnp.exp(s - m)
            den = jnp.sum(e, axis=-1, keepdims=True) + jnp.exp(sink - m)
            pr = (e / den).astype(BF16)
            o = jnp.dot(pr, v2, preferred_element_type=F32)
            o_ref[:, slab * LANES:(slab + 1) * LANES] = jnp.where(head0, o[:W], o[W:]).astype(o_ref.dtype)


def _swa_attention(qkv, sinks, d):
    t = qkv.shape[0]
    kvw = SWA_KV_HEADS * LANES
    qb = d // kvw
    group = d // HEAD // SWA_KV_HEADS
    assert group % 2 == 0 and d % kvw == 0 and t % WINDOW == 0
    prev = lambda n: jnp.maximum(n - 1, 0)
    return pl.pallas_call(
        functools.partial(_swa_kernel, group=group),
        grid=(t // WINDOW,),
        in_specs=[pl.BlockSpec(memory_space=pltpu.SMEM),
                  pl.BlockSpec((WINDOW, d), lambda n: (n, 0)),
                  pl.BlockSpec((WINDOW, kvw), lambda n: (n, qb)),
                  pl.BlockSpec((WINDOW, kvw), lambda n: (prev(n), qb)),
                  pl.BlockSpec((WINDOW, kvw), lambda n: (n, qb + 1)),
                  pl.BlockSpec((WINDOW, kvw), lambda n: (prev(n), qb + 1))],
        out_specs=pl.BlockSpec((WINDOW, d), lambda n: (n, 0)),
        out_shape=jax.ShapeDtypeStruct((t, d), BF16),
        compiler_params=_cparams(("arbitrary",)),
        name="swa_attention",
    )(sinks.astype(F32), qkv, qkv, qkv, qkv, qkv)


def _conv_kernel(u_ref, halo_ref, w_ref, b_ref, g_ref, beta_ref, o_ref, buf_ref, acc_ref):
    tm, d = u_ref.shape
    first = pl.program_id(0) == 0
    buf_ref[0:CONV_HALO, :] = jnp.where(first, 0.0, halo_ref[...])
    buf_ref[CONV_HALO:, :] = u_ref[...]
    rb = min(tm, 64)
    base = CONV_HALO - (CONV_WIDTH - 1)

    def col_body(c, carry):
        cs = pl.ds(pl.multiple_of(c * LANES, LANES), LANES)
        for r0 in range(0, tm, rb):
            acc = None
            for s in range(SUBLANES):
                rows = rb if s == 0 else rb + SUBLANES
                part = None
                for o in range(base, base + CONV_WIDTH):
                    if o % SUBLANES != s:
                        continue
                    term = buf_ref[pl.ds(r0 + o - s, rows), cs] * w_ref[pl.ds(o - base, 1), cs]
                    part = term if part is None else part + term
                if s:
                    part = pltpu.roll(part, shift=rows - s, axis=0)[:rb]
                acc = part if acc is None else acc + part
            acc_ref[pl.ds(r0, rb), cs] = acc
        return carry

    lax.fori_loop(0, d // LANES, col_body, 0)
    y = acc_ref[...] + b_ref[...]
    mu = jnp.mean(y, axis=-1, keepdims=True)
    yc = y - mu
    var = jnp.mean(yc * yc, axis=-1, keepdims=True)
    yn = yc * lax.rsqrt(var + LN_EPS) * g_ref[...] + beta_ref[...]
    o_ref[...] = (yn * jax.nn.sigmoid(yn)).astype(o_ref.dtype)


def _conv_ln_silu(u, w_dw, b_dw, ln_g, ln_b):
    t, d = u.shape
    tm = _tile(t, 256)
    assert tm % CONV_HALO == 0
    ratio = tm // CONV_HALO
    row = lambda a: a.reshape(1, d).astype(F32)
    vec = pl.BlockSpec((1, d), lambda i: (0, 0))
    return pl.pallas_call(
        _conv_kernel,
        grid=(t // tm,),
        in_specs=[pl.BlockSpec((tm, d), lambda i: (i, 0)),
                  pl.BlockSpec((CONV_HALO, d), lambda i: (jnp.maximum(i * ratio - 1, 0), 0)),
                  pl.BlockSpec((CONV_WIDTH, d), lambda i: (0, 0)),
                  vec, vec, vec],
        out_specs=pl.BlockSpec((tm, d), lambda i: (i, 0)),
        out_shape=jax.ShapeDtypeStruct((t, d), BF16),
        scratch_shapes=[pltpu.VMEM((tm + CONV_HALO, d), F32), pltpu.VMEM((tm, d), F32)],
        compiler_params=_cparams(("arbitrary",)),
        name="conv_dw_ln_silu",
    )(u, u, w_dw.astype(F32), row(b_dw), row(ln_g), row(ln_b))


def _pad_to(a, axis, mult):
    pad = (-a.shape[axis]) % mult
    if pad == 0:
        return a
    widths = [(0, 0)] * a.ndim
    widths[axis] = (0, pad)
    return jnp.pad(a, widths)


def _rwkv_layer(x, norm_g, v_first, vres, mix, w_rkv, layer, w0, w1, w2, a0, a1, a2, g1, g2,
                k_k, k_a, r_k, lnx_g, lnx_b, w_o):
    d = x.shape[1]
    branches = [(5, g1, g2, "sigmoid"), (3, w1, w2, "tanh"), (4, a1, a2, None)]
    if vres is not None:
        branches.append((2, vres[1], vres[2], None))
    downs = [_pad_to(b[1], 1, LANES) for b in branches]
    widths = [w.shape[1] for w in downs]
    offs = [sum(widths[:i]) for i in range(len(widths))]
    assert all(o % w == 0 for o, w in zip(offs, widths))
    w_down = jnp.concatenate(downs, axis=1).astype(BF16)
    loras = tuple((b[0], o, w, b[3]) for b, o, w in zip(branches, offs, widths))
    xs, mids = _rwkv_mix(x, norm_g, mix, w_down, loras)
    r, k, v = (_matmul(xs, w_rkv, x_index=c, w_index=3 * layer + c, name="rwkv_" + "rkv"[c])
               for c in range(3))
    ups = [_matmul(mids, _pad_to(b[2], 0, LANES).astype(BF16), x_col=o // w, kd=w,
                   out_dtype=F32 if b[0] == 3 else BF16, name="rwkv_lora_up")
           for b, o, w in zip(branches, offs, widths)]
    g, wl, al = ups[:3]
    if vres is None:
        v_first = v
        vres_args = None
    else:
        vres_args = (v_first, ups[3], vres[0])
    z = _wkv7(r, k, v, wl, al, g, w0, a0, k_k, k_a, r_k.reshape(d), lnx_g, lnx_b, vres=vres_args)
    return _matmul(z, w_o, w_index=layer, residual=x, name="rwkv_out"), v_first


def _swa_layer(x, norm_g, w_qkv, b_qkv, sinks, w_o, layer, b_o):
    d = x.shape[1]
    kvw = SWA_KV_HEADS * HEAD
    scale = HEAD ** -0.5

    def dup(a):
        a = a.reshape(a.shape[:-1] + (SWA_KV_HEADS, 1, HEAD))
        a = jnp.broadcast_to(a, a.shape[:-2] + (2, HEAD))
        return a.reshape(a.shape[:-3] + (2 * kvw,))

    w_ext = jnp.concatenate([w_qkv[:, :d] * scale, dup(w_qkv[:, d:d + kvw]), dup(w_qkv[:, d + kvw:])], axis=1)
    b_ext = jnp.concatenate([b_qkv[:d] * scale, dup(b_qkv[d:d + kvw]), dup(b_qkv[d + kvw:])], axis=0)
    h = _rmsnorm(x, norm_g, BF16)
    qkv = _matmul(h, w_ext.astype(BF16), bias=b_ext, out_dtype=BF16, name="swa_qkv")
    o = _swa_attention(qkv, sinks, d)
    return _matmul(o, w_o, w_index=layer, bias=b_o, residual=x, name="swa_out")


def _conv_layer(x, norm_g, w_pw1, layer, b_pw1, w_dw, b_dw, ln_g, ln_b, w_pw2, b_pw2):
    h = _rmsnorm(x, norm_g, BF16)
    u = _glu_matmul(h, w_pw1, layer, b_pw1)
    u = _conv_ln_silu(u, w_dw, b_dw, ln_g, ln_b)
    return _matmul(u, w_pw2, w_index=layer, bias=b_pw2, residual=x, name="conv_pw2")


def _mlp(x, norm_g, w_in, w_out, layer, w_in_bf16):
    h = _rmsnorm(x, norm_g, BF16)
    if w_in_bf16 is None:
        a, w_out_bf16 = _matmul(h, w_in, w_index=layer, act="relu2", out_dtype=BF16,
                                side_cast=(w_out, layer), name="mlp_in")
    else:
        a, w_out_bf16 = _matmul(h, w_in_bf16, act="relu2", out_dtype=BF16, tm=1024, tn=1024,
                                side_cast=(w_out, layer), name="mlp_in")
    tiles = dict(tm=512, tn=1024, tk=4096)
    if layer + 1 == w_in.shape[0]:
        return _matmul(a, w_out_bf16, residual=x, name="mlp_out", **tiles), None
    return _matmul(a, w_out_bf16, residual=x, side_cast=(w_in, layer + 1), name="mlp_out", **tiles)


def kernel(x, norm_mix_g, norm_mlp_g, norm_f_g, rwkv_mix, rwkv_w_rkv, rwkv_w0, rwkv_w1, rwkv_w2, rwkv_a0, rwkv_a1, rwkv_a2, rwkv_v0, rwkv_v1, rwkv_v2, rwkv_g1, rwkv_g2, rwkv_k_k, rwkv_k_a, rwkv_r_k, rwkv_lnx_g, rwkv_lnx_b, rwkv_w_o, swa_w_qkv, swa_b_qkv, swa_sinks, swa_w_o, swa_b_o, conv_w_pw1, conv_b_pw1, conv_w_dw, conv_b_dw, conv_ln_g, conv_ln_b, conv_w_pw2, conv_b_pw2, mlp_w_in, mlp_w_out):
    b, t, d = x.shape
    assert b == 1, "token shift and the WKV scan are written for a single sequence"
    depth = norm_mix_g.shape[0]
    xs = x.reshape(t, d)
    w_rkv = rwkv_w_rkv.reshape((-1,) + rwkv_w_rkv.shape[2:])
    ia = ib = ic = 0
    v_first = None
    w_in_bf16 = None
    for i in range(depth):
        kind = i % 3
        if kind == 0:
            vres = None if ia == 0 else (rwkv_v0[ia - 1], rwkv_v1[ia - 1], rwkv_v2[ia - 1])
            xs, v_first = _rwkv_layer(xs, norm_mix_g[i], v_first, vres, rwkv_mix[ia], w_rkv, ia,
                                      rwkv_w0[ia], rwkv_w1[ia], rwkv_w2[ia], rwkv_a0[ia], rwkv_a1[ia],
                                      rwkv_a2[ia], rwkv_g1[ia], rwkv_g2[ia], rwkv_k_k[ia], rwkv_k_a[ia],
                                      rwkv_r_k[ia], rwkv_lnx_g[ia], rwkv_lnx_b[ia], rwkv_w_o)
            ia += 1
        elif kind == 1:
            xs = _swa_layer(xs, norm_mix_g[i], swa_w_qkv[ib], swa_b_qkv[ib], swa_sinks[ib],
                            swa_w_o, ib, swa_b_o[ib])
            ib += 1
        else:
            xs = _conv_layer(xs, norm_mix_g[i], conv_w_pw1, ic, conv_b_pw1[ic], conv_w_dw[ic],
                             conv_b_dw[ic], conv_ln_g[ic], conv_ln_b[ic], conv_w_pw2, conv_b_pw2[ic])
            ic += 1
        xs, w_in_bf16 = _mlp(xs, norm_mlp_g[i], mlp_w_in, mlp_w_out, i, w_in_bf16)
    return _rmsnorm(xs, norm_f_g, F32).reshape(b, t, d)
```

```python
import functools

import jax
import jax.numpy as jnp
from jax import lax
from jax.experimental import pallas as pl
from jax.experimental.pallas import tpu as pltpu

F32 = jnp.float32
BF16 = jnp.bfloat16

LANES = 128
SUBLANES = 8
HEAD = 64
CHUNK = 128
WKV_PAIRS = 16
WINDOW = 128
SWA_KV_HEADS = 8
CONV_WIDTH = 31
CONV_HALO = 32
RMS_EPS = 1e-5
LN_EPS = 1e-5
GN_EPS = 64e-5
NEG_INF = -1e30
VMEM_LIMIT = 56 * 1024 * 1024


def _cparams(sem):
    return pltpu.CompilerParams(dimension_semantics=sem, vmem_limit_bytes=VMEM_LIMIT)


def _tile(n, pref):
    if n <= pref:
        return n
    t = pref
    while n % t:
        t //= 2
    return t


def _rmsnorm_kernel(x_ref, g_ref, o_ref):
    x = x_ref[...]
    y = x * lax.rsqrt(jnp.mean(x * x, axis=-1, keepdims=True) + RMS_EPS)
    o_ref[...] = (y * g_ref[...]).astype(o_ref.dtype)


def _rmsnorm(x, g, out_dtype):
    m, d = x.shape
    tm = _tile(m, 256)
    return pl.pallas_call(
        _rmsnorm_kernel,
        grid=(m // tm,),
        in_specs=[pl.BlockSpec((tm, d), lambda i: (i, 0)),
                  pl.BlockSpec((1, d), lambda i: (0, 0))],
        out_specs=pl.BlockSpec((tm, d), lambda i: (i, 0)),
        out_shape=jax.ShapeDtypeStruct((m, d), out_dtype),
        compiler_params=_cparams(("arbitrary",)),
        name="rmsnorm",
    )(x, g.reshape(1, d))


def _rwkv_mix_kernel(x_ref, g_ref, mix_ref, wd_ref, o_ref, mid_ref, carry_ref, *, loras):
    @pl.when(pl.program_id(0) == 0)
    def _():
        carry_ref[...] = jnp.zeros_like(carry_ref)

    x = x_ref[...]
    h = x * lax.rsqrt(jnp.mean(x * x, axis=-1, keepdims=True) + RMS_EPS) * g_ref[...]
    tm = h.shape[0]
    prev = pltpu.roll(h, shift=1, axis=0)
    row = lax.broadcasted_iota(jnp.int32, h.shape, 0)
    prev = jnp.where(row == 0, carry_ref[...], prev)
    carry_ref[...] = h[tm - 1:tm, :]
    xx = prev - h
    streams = {}

    def stream(c):
        if c not in streams:
            streams[c] = (h + xx * mix_ref[c:c + 1, :]).astype(BF16)
        return streams[c]

    for c in range(3):
        o_ref[c] = stream(c)
    for c, off, width, act in loras:
        mid = jnp.dot(stream(c), wd_ref[:, off:off + width], preferred_element_type=F32)
        mid_ref[:, off:off + width] = _activate(mid, act).astype(mid_ref.dtype)


def _rwkv_mix(x, g, mix, w_down, loras):
    m, d = x.shape
    nl = w_down.shape[1]
    tm = _tile(m, 256)
    return pl.pallas_call(
        functools.partial(_rwkv_mix_kernel, loras=loras),
        grid=(m // tm,),
        in_specs=[pl.BlockSpec((tm, d), lambda i: (i, 0)),
                  pl.BlockSpec((1, d), lambda i: (0, 0)),
                  pl.BlockSpec((6, d), lambda i: (0, 0)),
                  pl.BlockSpec((d, nl), lambda i: (0, 0))],
        out_specs=[pl.BlockSpec((3, tm, d), lambda i: (0, i, 0)),
                   pl.BlockSpec((tm, nl), lambda i: (i, 0))],
        out_shape=[jax.ShapeDtypeStruct((3, m, d), BF16), jax.ShapeDtypeStruct((m, nl), BF16)],
        scratch_shapes=[pltpu.VMEM((1, d), F32)],
        compiler_params=_cparams(("arbitrary",)),
        name="rwkv_mix",
    )(x, g.reshape(1, d), mix, w_down)


def _activate(acc, act):
    if act is None:
        return acc
    if act == "tanh":
        return jnp.tanh(acc)
    if act == "sigmoid":
        return jax.nn.sigmoid(acc)
    if act == "relu2":
        return jnp.square(jnp.maximum(acc, 0.0))
    raise ValueError(act)


def _mm_kernel(*refs, nk, act, has_bias, has_res, cast_w, n_side):
    x_ref, w_ref = refs[0], refs[1]
    pos = 2
    b_ref = r_ref = None
    if has_bias:
        b_ref = refs[pos]
        pos += 1
    if has_res:
        r_ref = refs[pos]
        pos += 1
    o_ref = refs[pos + n_side]
    for s in range(n_side):
        side_in_ref, side_out_ref = refs[pos + s], refs[pos + n_side + 1 + s]

        @pl.when(pl.program_id(2) == 0)
        def _(side_in_ref=side_in_ref, side_out_ref=side_out_ref):
            side_out_ref[...] = side_in_ref[...].astype(side_out_ref.dtype)

    scratch = list(refs[pos + 2 * n_side + 1:])

    def epilogue(acc):
        if has_bias:
            acc = acc + b_ref[...]
        acc = _activate(acc, act)
        if has_res:
            acc = acc + r_ref[...]
        o_ref[...] = acc.astype(o_ref.dtype)

    if cast_w:
        wb_ref = scratch.pop(0)

        @pl.when(pl.program_id(1) == 0)
        def _():
            wb_ref[...] = w_ref[...].astype(BF16)

        w = wb_ref[...]
    else:
        w = w_ref[...]
    part = jnp.dot(x_ref[...], w, preferred_element_type=F32)
    if nk == 1:
        epilogue(part)
        return
    acc_ref = scratch.pop(0)
    k = pl.program_id(2)

    @pl.when(k == 0)
    def _():
        acc_ref[...] = part

    @pl.when(k > 0)
    def _():
        acc_ref[...] += part

    @pl.when(k == nk - 1)
    def _():
        epilogue(acc_ref[...])


def _matmul(x, w, *, name, x_index=None, x_col=0, kd=None, w_index=None, bias=None, act=None,
            residual=None, out_dtype=F32, tm=None, tn=None, tk=4096, side_cast=None):
    m = x.shape[-2]
    kd = x.shape[-1] if kd is None else kd
    n = w.shape[-1]
    assert w.shape[-2] == kd
    cast_w = w.dtype != BF16
    if tn is None:
        tn = 512 if cast_w else 1024
    if tm is None:
        tm = 1024 if cast_w else 512
    tm, tn, tk = _tile(m, tm), _tile(n, tn), _tile(kd, tk)
    nk = kd // tk
    assert not cast_w or nk == 1, "the cast-once path needs the weight block fixed across row tiles"
    assert x_col == 0 or nk == 1
    if w.ndim == 3:
        w_spec = pl.BlockSpec((None, tk, tn), lambda j, i, k: (w_index, k, j))
    else:
        w_spec = pl.BlockSpec((tk, tn), lambda j, i, k: (k, j))
    if x.ndim == 3:
        x_spec = pl.BlockSpec((None, tm, tk), lambda j, i, k: (x_index, i, k + x_col))
    else:
        x_spec = pl.BlockSpec((tm, tk), lambda j, i, k: (i, k + x_col))
    in_specs = [x_spec, w_spec]
    args = [x, w]
    if bias is not None:
        in_specs.append(pl.BlockSpec((1, tn), lambda j, i, k: (0, j)))
        args.append(bias.reshape(1, n).astype(F32))
    if residual is not None:
        in_specs.append(pl.BlockSpec((tm, tn), lambda j, i, k: (i, j)))
        args.append(residual)
    grid = (n // tn, m // tm, nk)
    out_specs = pl.BlockSpec((tm, tn), lambda j, i, k: (i, j))
    out_shape = jax.ShapeDtypeStruct((m, n), out_dtype)
    side_cast = list(side_cast or ())
    if side_cast:
        out_specs, out_shape = [out_specs], [out_shape]
    for stack, s_index in side_cast:
        rows, cols = stack.shape[-2:]
        steps = grid[0] * grid[1]
        assert rows % (steps * 16) == 0
        slab = rows // steps
        in_specs.append(pl.BlockSpec((None, slab, cols),
                                     lambda j, i, k, s_index=s_index: (s_index, j * grid[1] + i, 0)))
        args.append(stack)
        out_specs.append(pl.BlockSpec((slab, cols), lambda j, i, k: (j * grid[1] + i, 0)))
        out_shape.append(jax.ShapeDtypeStruct((rows, cols), BF16))
    scratch = []
    if cast_w:
        scratch.append(pltpu.VMEM((tk, tn), BF16))
    if nk > 1:
        scratch.append(pltpu.VMEM((tm, tn), F32))
    return pl.pallas_call(
        functools.partial(_mm_kernel, nk=nk, act=act, has_bias=bias is not None,
                          has_res=residual is not None, cast_w=cast_w, n_side=len(side_cast)),
        grid=grid,
        in_specs=in_specs,
        out_specs=out_specs,
        out_shape=out_shape,
        scratch_shapes=scratch,
        compiler_params=_cparams(("arbitrary" if cast_w else "parallel",) * 2 + ("arbitrary",)),
        name=name,
    )(*args)


def _glu_mm_kernel(x_ref, wa_ref, wb_ref, ba_ref, bb_ref, o_ref, wab_ref, wbb_ref):
    @pl.when(pl.program_id(1) == 0)
    def _():
        wab_ref[...] = wa_ref[...].astype(BF16)
        wbb_ref[...] = wb_ref[...].astype(BF16)

    x = x_ref[...]
    a = jnp.dot(x, wab_ref[...], preferred_element_type=F32) + ba_ref[...]
    b = jnp.dot(x, wbb_ref[...], preferred_element_type=F32) + bb_ref[...]
    o_ref[...] = (a * jax.nn.sigmoid(b)).astype(o_ref.dtype)


def _glu_matmul(x, w, w_index, bias):
    m, kd = x.shape
    n = w.shape[-1] // 2
    tm, tn = _tile(m, 512), _tile(n, 256)
    nj = n // tn
    b2 = bias.reshape(1, 2 * n).astype(F32)
    return pl.pallas_call(
        _glu_mm_kernel,
        grid=(nj, m // tm),
        in_specs=[pl.BlockSpec((tm, kd), lambda j, i: (i, 0)),
                  pl.BlockSpec((None, kd, tn), lambda j, i: (w_index, 0, j)),
                  pl.BlockSpec((None, kd, tn), lambda j, i: (w_index, 0, j + nj)),
                  pl.BlockSpec((1, tn), lambda j, i: (0, j)),
                  pl.BlockSpec((1, tn), lambda j, i: (0, j + nj))],
        out_specs=pl.BlockSpec((tm, tn), lambda j, i: (i, j)),
        out_shape=jax.ShapeDtypeStruct((m, n), F32),
        scratch_shapes=[pltpu.VMEM((kd, tn), BF16), pltpu.VMEM((kd, tn), BF16)],
        compiler_params=_cparams(("arbitrary", "arbitrary")),
        name="conv_pw1_glu",
    )(x, w, w, b2, b2)


def _bdot(a, b):
    return jnp.dot(a.astype(BF16), b.astype(BF16), preferred_element_type=F32)


def _bdot_nt(a, b):
    return lax.dot_general(a.astype(BF16), b.astype(BF16), (((1,), (1,)), ((), ())),
                           preferred_element_type=F32)


def _split(a):
    hi = a.astype(BF16)
    lo = (a - hi.astype(F32)).astype(BF16)
    return hi, lo


def _blockdiag(y):
    left = lax.broadcasted_iota(jnp.int32, y.shape, 1) < y.shape[0]
    return jnp.concatenate([jnp.where(left, y, 0.0), jnp.where(left, 0.0, y)], axis=0)


def _hdot(x, y):
    return _bdot(x, _blockdiag(y))


def _unit_lower_inverses(n_list, eye2, xr2):
    n8 = [jnp.where(xr2 < 8, n, 0.0) for n in n_list]
    n8_2 = [_hdot(x, x) for x in n8]
    n8_4 = [_hdot(x, x) for x in n8_2]
    n8_3 = [_hdot(x, y) for x, y in zip(n8, n8_2)]
    t = [eye2 + a + b + c for a, b, c in zip(n8, n8_2, n8_3)]
    t = [x + _hdot(x, y) for x, y in zip(t, n8_4)]
    k = 8
    while k < CHUNK:
        level = (xr2 >= k) & (xr2 < 2 * k)
        u = [_hdot(jnp.where(level, n, 0.0), x) for n, x in zip(n_list, t)]
        t = [x + _hdot(x, y) for x, y in zip(t, u)]
        k *= 2
    return t


def _wkv_kernel(*refs, pairs, has_vres):
    if has_vres:
        (r_ref, k_ref, v_ref, wl_ref, al_ref, g_ref, vf_ref, vl_ref,
         w0_ref, a0_ref, kk_ref, ka_ref, rk_ref, lg_ref, lb_ref, v0_ref, o_ref, s_ref) = refs
    else:
        (r_ref, k_ref, v_ref, wl_ref, al_ref, g_ref,
         w0_ref, a0_ref, kk_ref, ka_ref, rk_ref, lg_ref, lb_ref, o_ref, s_ref) = refs
        vf_ref = vl_ref = v0_ref = None

    @pl.when(pl.program_id(1) == 0)
    def _():
        s_ref[...] = jnp.zeros_like(s_ref)

    L = CHUNK
    row = lax.broadcasted_iota(jnp.int32, (L, L), 0)
    col = lax.broadcasted_iota(jnp.int32, (L, L), 1)
    lower = row >= col
    same_head = (row >= HEAD) == (col >= HEAD)
    eye = (row == col).astype(F32)
    ltri = lower.astype(BF16)
    seg = same_head.astype(BF16)
    row2 = lax.broadcasted_iota(jnp.int32, (L, 2 * L), 0)
    col2 = lax.broadcasted_iota(jnp.int32, (L, 2 * L), 1) % L
    lower2 = row2 >= col2
    strict2 = row2 > col2
    xr2 = row2 ^ col2
    eye2 = (row2 == col2).astype(F32)

    def both(a):
        h0 = lax.broadcasted_iota(jnp.int32, a.shape, 1) % LANES < HEAD
        return jnp.concatenate([jnp.where(h0, a, 0.0), jnp.where(h0, 0.0, a)], axis=0)

    slabs = [slice(p * LANES, (p + 1) * LANES) for p in range(pairs)]

    def segsum(x):
        return _bdot(x, seg)

    def prepare(sl):
        r = r_ref[:, sl]
        k = k_ref[:, sl]
        v = v_ref[:, sl]
        z = w0_ref[:, sl] + wl_ref[:, sl]
        w = jnp.minimum(z, 0.0) - jnp.log(1.0 + jnp.exp(-jnp.abs(z))) - 0.5
        lw = -jnp.exp(w)
        a = jax.nn.sigmoid(a0_ref[:, sl] + al_ref[:, sl])
        if has_vres:
            v = v + (vf_ref[:, sl] - v) * jax.nn.sigmoid(v0_ref[:, sl] + vl_ref[:, sl])
        kk = k * kk_ref[:, sl]
        kk = kk / jnp.maximum(jnp.sqrt(segsum(kk * kk)), 1e-12)
        k = k * (1.0 + (a - 1.0) * ka_ref[:, sl])
        avec = -kk
        bvec = kk * a
        lw_hi, lw_lo = _split(lw)
        cum = (jnp.dot(ltri, lw_hi, preferred_element_type=F32)
               + jnp.dot(ltri, lw_lo, preferred_element_type=F32))
        cmid = cum[L // 2 - 1:L // 2, :]
        clast = cum[L - 1:L, :]
        e_neg = jnp.exp(cmid - cum)
        e_last = jnp.exp(clast - cum)
        return dict(
            v=v, rkb=r * k * rk_ref[:, sl],
            p_last=jnp.exp(clast), p_mid=jnp.exp(cmid),
            r_h=r * jnp.exp(cum - cmid),
            a_h=avec * jnp.exp(cum - lw - cmid),
            b_t=bvec * e_neg, k_t=k * e_neg, b_d=bvec * e_last, k_d=k * e_last)

    states = [s_ref[p] for p in range(pairs)]

    def sbs(att, r0, c0):
        return jnp.concatenate([att[r0:r0 + L, c0:c0 + L], att[r0 + 2 * L:r0 + 3 * L, c0:c0 + L]], axis=1)

    def interactions(q):
        return _bdot_nt(both(jnp.concatenate([q["a_h"], q["r_h"]], axis=0)),
                        jnp.concatenate([q["b_t"], q["k_t"]], axis=0))

    half = max(pairs // 2, 1)
    pre = [prepare(sl) for sl in slabs[:half]]
    att = [interactions(q) for q in pre]
    pre += [prepare(sl) for sl in slabs[half:]]
    a_ab = [jnp.where(strict2, sbs(x, 0, 0), 0.0) for x in att]
    t_inv = _unit_lower_inverses(a_ab, eye2, xr2)
    att += [interactions(q) for q in pre[half:]]
    a_ab += [jnp.where(strict2, sbs(x, 0, 0), 0.0) for x in att[half:]]
    t_inv += _unit_lower_inverses(a_ab[half:], eye2, xr2)
    a_ak = [jnp.where(strict2, sbs(x, 0, L), 0.0) for x in att]
    a_rb = [jnp.where(lower2, sbs(x, L, 0), 0.0) for x in att]
    a_rk = [jnp.where(lower2, sbs(x, L, L), 0.0) for x in att]

    v2 = [both(q["v"]) for q in pre]
    akv = [_bdot(x, y) for x, y in zip(a_ak, v2)]
    wu = [_bdot(t, both(jnp.concatenate([q["a_h"], u], axis=1)))
          for t, u, q in zip(t_inv, akv, pre)]
    qy = [_bdot(x, both(y)) for x, y in zip(a_rb, wu)]
    yk = [_bdot(x, y) for x, y in zip(a_rk, v2)]

    q_t = [(q["r_h"] + x[:, :LANES]) * q["p_mid"] for q, x in zip(pre, qy)]
    y_in = [x[:, LANES:] + z for x, z in zip(qy, yk)]
    bdt = [q["b_d"].T for q in pre]
    kdt = [q["k_d"].T for q in pre]
    m2 = [eye * q["p_last"] + jnp.where(same_head, _bdot(b, x[:, :LANES] * q["p_mid"]), 0.0)
          for q, b, x in zip(pre, bdt, wu)]
    g2 = [jnp.where(same_head, _bdot(jnp.concatenate([b, kd], axis=1),
                                     jnp.concatenate([x[:, LANES:], q["v"]], axis=0)), 0.0)
          for q, b, kd, x in zip(pre, bdt, kdt, wu)]
    y = [_bdot(x, s) + z for x, s, z in zip(q_t, states, y_in)]
    new_states = [_bdot(m, s) + g for m, s, g in zip(m2, states, g2)]

    d = [x - segsum(x) * (1.0 / HEAD) for x in y]
    var = [segsum(x * x) * (1.0 / HEAD) for x in d]
    bonus = [segsum(q["rkb"]) * q["v"] for q in pre]
    for p, sl in enumerate(slabs):
        s_ref[p] = new_states[p]
        yn = d[p] * lax.rsqrt(var[p] + GN_EPS) * lg_ref[:, sl] + lb_ref[:, sl]
        o_ref[:, sl] = ((yn + bonus[p]) * g_ref[:, sl]).astype(o_ref.dtype)


def _wkv7(r, k, v, wl, al, g, w0, a0, k_k, k_a, r_k, lnx_g, lnx_b, vres=None):
    t, d = r.shape
    pairs = min(WKV_PAIRS, d // LANES)
    pw = pairs * LANES
    assert t % CHUNK == 0 and d % pw == 0
    seq = pl.BlockSpec((CHUNK, pw), lambda j, c: (c, j))
    par = pl.BlockSpec((1, pw), lambda j, c: (0, j))
    row = lambda a: a.reshape(1, d).astype(F32)
    seq_args = [r, k, v, wl, al, g]
    par_args = [row(w0), row(a0), row(k_k), row(k_a), row(r_k), row(lnx_g), row(lnx_b)]
    if vres is not None:
        v_first, vl, v0 = vres
        seq_args += [v_first, vl]
        par_args += [row(v0)]
    return pl.pallas_call(
        functools.partial(_wkv_kernel, pairs=pairs, has_vres=vres is not None),
        grid=(d // pw, t // CHUNK),
        in_specs=[seq] * len(seq_args) + [par] * len(par_args),
        out_specs=seq,
        out_shape=jax.ShapeDtypeStruct((t, d), BF16),
        scratch_shapes=[pltpu.VMEM((pairs, LANES, LANES), F32)],
        compiler_params=_cparams(("parallel", "arbitrary")),
        name="wkv7",
    )(*seq_args, *par_args)


def _swa_kernel(sink_ref, q_ref, kc_ref, kp_ref, vc_ref, vp_ref, o_ref, *, group):
    n = pl.program_id(0)
    W = WINDOW
    qi = lax.broadcasted_iota(jnp.int32, (2 * W, 2 * W), 0) % W
    kj = lax.broadcasted_iota(jnp.int32, (2 * W, 2 * W), 1)
    valid = (kj > qi) & (kj <= qi + W) & ((kj >= W) | (n > 0))
    top = lax.broadcasted_iota(jnp.int32, (2 * W, 1), 0) < W
    lane = lax.broadcasted_iota(jnp.int32, (W, LANES), 1)
    head0 = lane < HEAD
    zero = jnp.zeros((), BF16)
    for hk in range(SWA_KV_HEADS):
        ks = slice(hk * LANES, (hk + 1) * LANES)
        k2 = jnp.concatenate([kp_ref[:, ks], kc_ref[:, ks]], axis=0)
        v2 = jnp.concatenate([vp_ref[:, ks], vc_ref[:, ks]], axis=0)
        for i in range(group // 2):
            slab = hk * (group // 2) + i
            qs = q_ref[:, slab * LANES:(slab + 1) * LANES]
            q2 = jnp.concatenate([jnp.where(head0, qs, zero), jnp.where(head0, zero, qs)], axis=0)
            s = lax.dot_general(q2, k2, (((1,), (1,)), ((), ())), preferred_element_type=F32)
            s = jnp.where(valid, s, NEG_INF)
            sink = jnp.where(top, sink_ref[2 * slab], sink_ref[2 * slab + 1])
            m = jnp.maximum(jnp.max(s, axis=-1, keepdims=True), sink)
            e = jnp.exp(s - m)
            den = jnp.sum(e, axis=-1, keepdims=True) + jnp.exp(sink - m)
            pr = (e / den).astype(BF16)
            o = jnp.dot(pr, v2, preferred_element_type=F32)
            o_ref[:, slab * LANES:(slab + 1) * LANES] = jnp.where(head0, o[:W], o[W:]).astype(o_ref.dtype)


def _swa_attention(qkv, sinks, d):
    t = qkv.shape[0]
    kvw = SWA_KV_HEADS * LANES
    qb = d // kvw
    group = d // HEAD // SWA_KV_HEADS
    assert group % 2 == 0 and d % kvw == 0 and t % WINDOW == 0
    prev = lambda n: jnp.maximum(n - 1, 0)
    return pl.pallas_call(
        functools.partial(_swa_kernel, group=group),
        grid=(t // WINDOW,),
        in_specs=[pl.BlockSpec(memory_space=pltpu.SMEM),
                  pl.BlockSpec((WINDOW, d), lambda n: (n, 0)),
                  pl.BlockSpec((WINDOW, kvw), lambda n: (n, qb)),
                  pl.BlockSpec((WINDOW, kvw), lambda n: (prev(n), qb)),
                  pl.BlockSpec((WINDOW, kvw), lambda n: (n, qb + 1)),
                  pl.BlockSpec((WINDOW, kvw), lambda n: (prev(n), qb + 1))],
        out_specs=pl.BlockSpec((WINDOW, d), lambda n: (n, 0)),
        out_shape=jax.ShapeDtypeStruct((t, d), BF16),
        compiler_params=_cparams(("arbitrary",)),
        name="swa_attention",
    )(sinks.astype(F32), qkv, qkv, qkv, qkv, qkv)


def _conv_kernel(u_ref, halo_ref, w_ref, b_ref, g_ref, beta_ref, o_ref, buf_ref, acc_ref):
    tm, d = u_ref.shape
    first = pl.program_id(0) == 0
    buf_ref[0:CONV_HALO, :] = jnp.where(first, 0.0, halo_ref[...])
    buf_ref[CONV_HALO:, :] = u_ref[...]
    rb = min(tm, 64)
    base = CONV_HALO - (CONV_WIDTH - 1)

    def col_body(c, carry):
        cs = pl.ds(pl.multiple_of(c * LANES, LANES), LANES)
        for r0 in range(0, tm, rb):
            acc = None
            for s in range(SUBLANES):
                rows = rb if s == 0 else rb + SUBLANES
                part = None
                for o in range(base, base + CONV_WIDTH):
                    if o % SUBLANES != s:
                        continue
                    term = buf_ref[pl.ds(r0 + o - s, rows), cs] * w_ref[pl.ds(o - base, 1), cs]
                    part = term if part is None else part + term
                if s:
                    part = pltpu.roll(part, shift=rows - s, axis=0)[:rb]
                acc = part if acc is None else acc + part
            acc_ref[pl.ds(r0, rb), cs] = acc
        return carry

    lax.fori_loop(0, d // LANES, col_body, 0)
    y = acc_ref[...] + b_ref[...]
    mu = jnp.mean(y, axis=-1, keepdims=True)
    yc = y - mu
    var = jnp.mean(yc * yc, axis=-1, keepdims=True)
    yn = yc * lax.rsqrt(var + LN_EPS) * g_ref[...] + beta_ref[...]
    o_ref[...] = (yn * jax.nn.sigmoid(yn)).astype(o_ref.dtype)


def _conv_ln_silu(u, w_dw, b_dw, ln_g, ln_b):
    t, d = u.shape
    tm = _tile(t, 256)
    assert tm % CONV_HALO == 0
    ratio = tm // CONV_HALO
    row = lambda a: a.reshape(1, d).astype(F32)
    vec = pl.BlockSpec((1, d), lambda i: (0, 0))
    return pl.pallas_call(
        _conv_kernel,
        grid=(t // tm,),
        in_specs=[pl.BlockSpec((tm, d), lambda i: (i, 0)),
                  pl.BlockSpec((CONV_HALO, d), lambda i: (jnp.maximum(i * ratio - 1, 0), 0)),
                  pl.BlockSpec((CONV_WIDTH, d), lambda i: (0, 0)),
                  vec, vec, vec],
        out_specs=pl.BlockSpec((tm, d), lambda i: (i, 0)),
        out_shape=jax.ShapeDtypeStruct((t, d), BF16),
        scratch_shapes=[pltpu.VMEM((tm + CONV_HALO, d), F32), pltpu.VMEM((tm, d), F32)],
        compiler_params=_cparams(("arbitrary",)),
        name="conv_dw_ln_silu",
    )(u, u, w_dw.astype(F32), row(b_dw), row(ln_g), row(ln_b))


def _pad_to(a, axis, mult):
    pad = (-a.shape[axis]) % mult
    if pad == 0:
        return a
    widths = [(0, 0)] * a.ndim
    widths[axis] = (0, pad)
    return jnp.pad(a, widths)


def _rwkv_layer(x, norm_g, v_first, vres, mix, w_rkv, layer, w0, w1, w2, a0, a1, a2, g1, g2,
                k_k, k_a, r_k, lnx_g, lnx_b, w_o):
    d = x.shape[1]
    branches = [(5, g1, g2, "sigmoid"), (3, w1, w2, "tanh"), (4, a1, a2, None)]
    if vres is not None:
        branches.append((2, vres[1], vres[2], None))
    downs = [_pad_to(b[1], 1, LANES) for b in branches]
    widths = [w.shape[1] for w in downs]
    offs = [sum(widths[:i]) for i in range(len(widths))]
    assert all(o % w == 0 for o, w in zip(offs, widths))
    w_down = jnp.concatenate(downs, axis=1).astype(BF16)
    loras = tuple((b[0], o, w, b[3]) for b, o, w in zip(branches, offs, widths))
    xs, mids = _rwkv_mix(x, norm_g, mix, w_down, loras)
    r, k, v = (_matmul(xs, w_rkv, x_index=c, w_index=3 * layer + c, name="rwkv_" + "rkv"[c])
               for c in range(3))
    ups = [_matmul(mids, _pad_to(b[2], 0, LANES).astype(BF16), x_col=o // w, kd=w,
                   out_dtype=F32 if b[0] == 3 else BF16, name="rwkv_lora_up")
           for b, o, w in zip(branches, offs, widths)]
    g, wl, al = ups[:3]
    if vres is None:
        v_first = v
        vres_args = None
    else:
        vres_args = (v_first, ups[3], vres[0])
    z = _wkv7(r, k, v, wl, al, g, w0, a0, k_k, k_a, r_k.reshape(d), lnx_g, lnx_b, vres=vres_args)
    return _matmul(z, w_o, w_index=layer, residual=x, name="rwkv_out"), v_first


def _swa_layer(x, norm_g, w_qkv, b_qkv, sinks, w_o, layer, b_o):
    d = x.shape[1]
    kvw = SWA_KV_HEADS * HEAD
    scale = HEAD ** -0.5

    def dup(a):
        a = a.reshape(a.shape[:-1] + (SWA_KV_HEADS, 1, HEAD))
        a = jnp.broadcast_to(a, a.shape[:-2] + (2, HEAD))
        return a.reshape(a.shape[:-3] + (2 * kvw,))

    w_ext = jnp.concatenate([w_qkv[:, :d] * scale, dup(w_qkv[:, d:d + kvw]), dup(w_qkv[:, d + kvw:])], axis=1)
    b_ext = jnp.concatenate([b_qkv[:d] * scale, dup(b_qkv[d:d + kvw]), dup(b_qkv[d + kvw:])], axis=0)
    h = _rmsnorm(x, norm_g, BF16)
    qkv = _matmul(h, w_ext.astype(BF16), bias=b_ext, out_dtype=BF16, name="swa_qkv")
    o = _swa_attention(qkv, sinks, d)
    return _matmul(o, w_o, w_index=layer, bias=b_o, residual=x, name="swa_out")


def _conv_layer(x, norm_g, w_pw1, layer, b_pw1, w_dw, b_dw, ln_g, ln_b, w_pw2, b_pw2):
    h = _rmsnorm(x, norm_g, BF16)
    u = _glu_matmul(h, w_pw1, layer, b_pw1)
    u = _conv_ln_silu(u, w_dw, b_dw, ln_g, ln_b)
    return _matmul(u, w_pw2, w_index=layer, bias=b_pw2, residual=x, name="conv_pw2")


def _mlp(x, norm_g, w_in, w_out, layer, w_in_bf16):
    h = _rmsnorm(x, norm_g, BF16)
    side = [(w_out, layer)]
    if layer + 1 < w_in.shape[0]:
        side.append((w_in, layer + 1))
    if w_in_bf16 is None:
        outs = _matmul(h, w_in, w_index=layer, act="relu2", out_dtype=BF16, side_cast=side, name="mlp_in")
    else:
        outs = _matmul(h, w_in_bf16, act="relu2", out_dtype=BF16, tm=1024, tn=1024, side_cast=side,
                       name="mlp_in")
    a, w_out_bf16 = outs[0], outs[1]
    next_w_in_bf16 = outs[2] if len(outs) > 2 else None
    return _matmul(a, w_out_bf16, residual=x, tm=512, tn=1024, tk=4096, name="mlp_out"), next_w_in_bf16


def kernel(x, norm_mix_g, norm_mlp_g, norm_f_g, rwkv_mix, rwkv_w_rkv, rwkv_w0, rwkv_w1, rwkv_w2, rwkv_a0, rwkv_a1, rwkv_a2, rwkv_v0, rwkv_v1, rwkv_v2, rwkv_g1, rwkv_g2, rwkv_k_k, rwkv_k_a, rwkv_r_k, rwkv_lnx_g, rwkv_lnx_b, rwkv_w_o, swa_w_qkv, swa_b_qkv, swa_sinks, swa_w_o, swa_b_o, conv_w_pw1, conv_b_pw1, conv_w_dw, conv_b_dw, conv_ln_g, conv_ln_b, conv_w_pw2, conv_b_pw2, mlp_w_in, mlp_w_out):
    b, t, d = x.shape
    assert b == 1, "token shift and the WKV scan are written for a single sequence"
    depth = norm_mix_g.shape[0]
    xs = x.reshape(t, d)
    w_rkv = rwkv_w_rkv.reshape((-1,) + rwkv_w_rkv.shape[2:])
    ia = ib = ic = 0
    v_first = None
    w_in_bf16 = None
    for i in range(depth):
        kind = i % 3
        if kind == 0:
            vres = None if ia == 0 else (rwkv_v0[ia - 1], rwkv_v1[ia - 1], rwkv_v2[ia - 1])
            xs, v_first = _rwkv_layer(xs, norm_mix_g[i], v_first, vres, rwkv_mix[ia], w_rkv, ia,
                                      rwkv_w0[ia], rwkv_w1[ia], rwkv_w2[ia], rwkv_a0[ia], rwkv_a1[ia],
                                      rwkv_a2[ia], rwkv_g1[ia], rwkv_g2[ia], rwkv_k_k[ia], rwkv_k_a[ia],
                                      rwkv_r_k[ia], rwkv_lnx_g[ia], rwkv_lnx_b[ia], rwkv_w_o)
            ia += 1
        elif kind == 1:
            xs = _swa_layer(xs, norm_mix_g[i], swa_w_qkv[ib], swa_b_qkv[ib], swa_sinks[ib],
                            swa_w_o, ib, swa_b_o[ib])
            ib += 1
        else:
            xs = _conv_layer(xs, norm_mix_g[i], conv_w_pw1, ic, conv_b_pw1[ic], conv_w_dw[ic],
                             conv_b_dw[ic], conv_ln_g[ic], conv_ln_b[ic], conv_w_pw2, conv_b_pw2[ic])
            ic += 1
        xs, w_in_bf16 = _mlp(xs, norm_mlp_g[i], mlp_w_in, mlp_w_out, i, w_in_bf16)
    return _rmsnorm(xs, norm_f_g, F32).reshape(b, t, d)
```
